```python
import math
import jax, jax.numpy as jnp
from jax import lax
import numpy as np

D_MODEL = 1024
BATCH = 2
SEQ = 8192
DEPTH = 1

SG_CHUNK = 128
SG_WIDTH = D_MODEL
SG_GROUPS = 8
SG_GROUP_DIM = SG_WIDTH // SG_GROUPS
GLA_HEADS = 4
GLA_KEY_DIM = D_MODEL // 2
GLA_VAL_DIM = D_MODEL
GLA_HEAD_K = GLA_KEY_DIM // GLA_HEADS
GLA_HEAD_V = GLA_VAL_DIM // GLA_HEADS
GLA_GATE_RANK = 16
GLA_GATE_TEMP = 16.0
GLA_CHUNK = 64
D_FF = 4 * D_MODEL
LN_EPS = 1e-5
DEEPNORM_ALPHA = (2.0 * DEPTH) ** 0.25
DEEPNORM_BETA = (8.0 * DEPTH) ** -0.25
SPLITS = (SG_WIDTH, SG_WIDTH, GLA_KEY_DIM, GLA_KEY_DIM, GLA_VAL_DIM, GLA_VAL_DIM,
          GLA_GATE_RANK, D_MODEL, D_MODEL)
D_IN = sum(SPLITS)
SPLIT_POINTS = [int(s) for s in np.cumsum(SPLITS)[:-1]]

kernel_name = 'hybrid_sgu_gla_deepnorm_block'


def layer_norm(x, g, b):
    xf = x.astype(jnp.float32)
    mu = jnp.mean(xf, axis=-1, keepdims=True)
    var = jnp.mean(jnp.square(xf - mu), axis=-1, keepdims=True)
    y = (xf - mu) * lax.rsqrt(var + LN_EPS) * g.astype(jnp.float32) + b.astype(jnp.float32)
    return y.astype(x.dtype)


def spatial_gating(u, v, ln_g, ln_b, w_s, b_s):
    bsz, t, _ = v.shape
    n = t // SG_CHUNK
    u = jax.nn.gelu(u)
    v = layer_norm(jax.nn.gelu(v), ln_g, ln_b)
    vc = v.reshape(bsz, n, SG_CHUNK, SG_GROUPS, SG_GROUP_DIM)
    causal = jnp.tril(jnp.ones((SG_CHUNK, SG_CHUNK), dtype=bool))
    ws = jnp.where(causal[None], w_s, jnp.zeros_like(w_s))
    mixed = jnp.einsum('gts,bnsgd->bntgd', ws, vc) + jnp.transpose(b_s)[:, :, None]
    return u * mixed.reshape(bsz, t, SG_WIDTH)


def gla_chunked(q, k, v, log_f):
    bsz, t, h, dk = q.shape
    dv = v.shape[-1]
    n = t // GLA_CHUNK

    def to_chunks(a):
        return a.astype(jnp.float32).reshape(bsz, n, GLA_CHUNK, h, a.shape[-1]).transpose(1, 0, 3, 2, 4)

    qc, kc, vc, gc = to_chunks(q * (GLA_HEAD_K ** -0.5)), to_chunks(k), to_chunks(v), to_chunks(log_f)
    causal = jnp.tril(jnp.ones((GLA_CHUNK, GLA_CHUNK), dtype=bool))[:, :, None]

    def step(state, inp):
        qb, kb, vb, gb = inp
        cum = jnp.cumsum(gb, axis=-2)
        diff = cum[..., :, None, :] - cum[..., None, :, :]
        decay = jnp.exp(jnp.where(causal, diff, -jnp.inf))
        scores = jnp.einsum('bhtd,bhsd,bhtsd->bhts', qb, kb, decay)
        o_intra = jnp.einsum('bhts,bhsv->bhtv', scores, vb)
        o_inter = jnp.einsum('bhtd,bhdv->bhtv', qb * jnp.exp(cum), state)
        total = cum[..., -1, :]
        k_dec = kb * jnp.exp(total[..., None, :] - cum)
        new_state = jnp.exp(total)[..., None] * state + jnp.einsum('bhsd,bhsv->bhdv', k_dec, vb)
        return new_state, o_intra + o_inter

    s0 = jnp.zeros((bsz, h, dk, dv), dtype=jnp.float32)
    _, out = lax.scan(step, s0, (qc, kc, vc, gc))
    return out.transpose(1, 0, 3, 2, 4).reshape(bsz, t, h, dv)


def setup_inputs(seed: int = 0) -> dict:
    key = jax.random.key(seed)
    ks = jax.random.split(key, 20)
    f32 = jnp.float32
    x = jax.random.normal(ks[0], (BATCH, SEQ, D_MODEL), f32)
    w_in = jax.random.normal(ks[1], (DEPTH, D_MODEL, D_IN), f32) * D_MODEL ** -0.5
    col_scale = np.ones((D_IN,), np.float32)
    col_scale[:2 * SG_WIDTH] = DEEPNORM_BETA
    v0 = 2 * SG_WIDTH + 2 * GLA_KEY_DIM
    col_scale[v0:v0 + GLA_VAL_DIM] = DEEPNORM_BETA
    w_in = w_in * jnp.asarray(col_scale)
    b_in = 0.02 * jax.random.normal(ks[2], (DEPTH, D_IN), f32)
    sg_ln_g = 1.0 + 0.05 * jax.random.normal(ks[3], (DEPTH, SG_WIDTH), f32)
    sg_ln_b = 0.02 * jax.random.normal(ks[4], (DEPTH, SG_WIDTH), f32)
    sg_w_s = 0.5 * jax.random.normal(ks[5], (DEPTH, SG_GROUPS, SG_CHUNK, SG_CHUNK), f32) * SG_CHUNK ** -0.5
    sg_b_s = 1.0 + 0.1 * jax.random.normal(ks[6], (DEPTH, SG_GROUPS, SG_CHUNK), f32)
    gla_w_gate2 = jax.random.normal(ks[7], (DEPTH, GLA_GATE_RANK, GLA_KEY_DIM), f32) * GLA_GATE_RANK ** -0.5
    gla_b_gate = 0.1 * jax.random.normal(ks[8], (DEPTH, GLA_KEY_DIM), f32)
    gla_norm_g = 1.0 + 0.05 * jax.random.normal(ks[9], (DEPTH, GLA_VAL_DIM), f32)
    w_out = jax.random.normal(ks[10], (DEPTH, D_MODEL, D_MODEL), f32) * (D_MODEL ** -0.5 * DEEPNORM_BETA)
    ln1_g = 1.0 + 0.05 * jax.random.normal(ks[11], (DEPTH, D_MODEL), f32)
    ln1_b = 0.02 * jax.random.normal(ks[12], (DEPTH, D_MODEL), f32)
    w_ff1 = jax.random.normal(ks[13], (DEPTH, D_MODEL, D_FF), f32) * (D_MODEL ** -0.5 * DEEPNORM_BETA)
    w_ff2 = jax.random.normal(ks[14], (DEPTH, D_FF, D_MODEL), f32) * (D_FF ** -0.5 * DEEPNORM_BETA)
    ln2_g = 1.0 + 0.05 * jax.random.normal(ks[15], (DEPTH, D_MODEL), f32)
    ln2_b = 0.02 * jax.random.normal(ks[16], (DEPTH, D_MODEL), f32)
    return {'x': x, 'w_in': w_in, 'b_in': b_in, 'sg_ln_g': sg_ln_g, 'sg_ln_b': sg_ln_b,
            'sg_w_s': sg_w_s, 'sg_b_s': sg_b_s, 'gla_w_gate2': gla_w_gate2, 'gla_b_gate': gla_b_gate,
            'gla_norm_g': gla_norm_g, 'w_out': w_out, 'ln1_g': ln1_g, 'ln1_b': ln1_b,
            'w_ff1': w_ff1, 'w_ff2': w_ff2, 'ln2_g': ln2_g, 'ln2_b': ln2_b}


def reference(x, w_in, b_in, sg_ln_g, sg_ln_b, sg_w_s, sg_b_s, gla_w_gate2, gla_b_gate,
              gla_norm_g, w_out, ln1_g, ln1_b, w_ff1, w_ff2, ln2_g, ln2_b):
    bsz, t, _ = x.shape
    h = x
    for l in range(DEPTH):
        p = jnp.einsum('btd,de->bte', h, w_in[l]) + b_in[l]
        a_u, a_v, q, k, v, r, a_low, g_a, g_b = jnp.split(p, SPLIT_POINTS, axis=-1)
        y_a = spatial_gating(a_u, a_v, sg_ln_g[l], sg_ln_b[l], sg_w_s[l], sg_b_s[l])
        log_f = jax.nn.log_sigmoid((a_low @ gla_w_gate2[l] + gla_b_gate[l]).astype(jnp.float32)) / GLA_GATE_TEMP
        o = gla_chunked(q.reshape(bsz, t, GLA_HEADS, GLA_HEAD_K),
                        k.reshape(bsz, t, GLA_HEADS, GLA_HEAD_K),
                        v.reshape(bsz, t, GLA_HEADS, GLA_HEAD_V),
                        log_f.reshape(bsz, t, GLA_HEADS, GLA_HEAD_K))
        o = o * lax.rsqrt(jnp.mean(jnp.square(o), axis=-1, keepdims=True) + LN_EPS)
        y_b = (o.reshape(bsz, t, GLA_VAL_DIM) * gla_norm_g[l].astype(jnp.float32)).astype(h.dtype) * jax.nn.silu(r)
        m = jax.nn.sigmoid(g_a) * y_a + jax.nn.sigmoid(g_b) * y_b
        h = layer_norm(DEEPNORM_ALPHA * h + m @ w_out[l], ln1_g[l], ln1_b[l])
        f = jnp.square(jax.nn.relu(h @ w_ff1[l])) @ w_ff2[l]
        h = layer_norm(DEEPNORM_ALPHA * h + f, ln2_g[l], ln2_b[l])
    return h
```

```python
import functools
import math

import jax
import jax.numpy as jnp
import numpy as np
from jax.experimental import pallas as pl
from jax.experimental.pallas import tpu as pltpu

D_MODEL = 1024
SG_CHUNK = 128
SG_GROUPS = 8
GLA_HEADS = 4
GLA_KEY_DIM = 512
GLA_VAL_DIM = 1024
GLA_HEAD_K = GLA_KEY_DIM // GLA_HEADS
GLA_HEAD_V = GLA_VAL_DIM // GLA_HEADS
GLA_GATE_RANK = 16
GLA_GATE_TEMP = 16.0
D_FF = 4 * D_MODEL
LN_EPS = 1e-5

LANES = 128
GLA_CHUNK = 128
GLA_LEVELS = int(math.log2(GLA_CHUNK))
MIX_BLOCK = 256
FFN_BLOCK = 512
VMEM_LIMIT = 56 * 1024 * 1024

OFF_U, OFF_V, OFF_Q, OFF_K, OFF_GV, OFF_R, OFF_GA, OFF_GB = 0, 1024, 2048, 2560, 3072, 4096, 5120, 6144
D_MAIN = 7168

BF16 = jnp.bfloat16
F32 = jnp.float32


def _decay_tables():
    c = GLA_CHUNK
    blocks = []
    for lw in range(GLA_LEVELS - 1, -1, -1):
        a = np.zeros((c, c), np.float32)
        for t in range(c):
            if (t >> lw) & 1:
                a[t, (t >> lw) << lw:t + 1] = 1.0
            else:
                a[t, t + 1:((t >> lw) + 1) << lw] = 1.0
        blocks.append(a)
    blocks.append(np.tril(np.ones((c, c), np.float32)))
    blocks.append(np.triu(np.ones((c, c), np.float32), k=1))
    table = np.concatenate(blocks, axis=0)
    t = np.arange(c)[:, None]
    s = np.arange(c)[None, :]
    x = t ^ s
    top = np.zeros_like(x)
    nz = x > 0
    top[nz] = np.floor(np.log2(x[nz])).astype(x.dtype)
    level = np.where(t > s, GLA_LEVELS - 1 - top, np.where(t == s, GLA_LEVELS, -1)).astype(np.int32)
    return table, level


def _layer_norm(x, g, b):
    mu = jnp.mean(x, axis=-1, keepdims=True)
    xc = x - mu
    var = jnp.mean(xc * xc, axis=-1, keepdims=True)
    return xc * jax.lax.rsqrt(var + LN_EPS) * g + b


def _gelu(x):
    return 0.5 * x * (1.0 + jnp.tanh(0.7978845608028654 * (x + 0.044715 * (x * x * x))))


def _sigmoid(x):
    return 0.5 * jnp.tanh(0.5 * x) + 0.5


def _split_bf16(x):
    hi = x.astype(BF16)
    lo = (x - hi.astype(F32)).astype(BF16)
    return hi, lo


def _dot(a, b):
    return jnp.dot(a, b, preferred_element_type=F32)


def _dot_nt(a, b):
    return jax.lax.dot_general(a, b, (((1,), (1,)), ((), ())), preferred_element_type=F32)


def _dot_tn(a, b):
    return jax.lax.dot_general(a, b, (((0,), (0,)), ((), ())), preferred_element_type=F32)


def _mixer_kernel(x_ref, w_ref, b_ref, wlow_ref, blow_ref, wg_hi_ref, wg_lo_ref, bg_ref,
                  sg_g_ref, sg_b_ref, ws_ref, bs_ref, tab_ref, lvl_ref, gn_ref,
                  wout_ref, ln_g_ref, ln_b_ref, o_ref, state_ref, *, alpha):
    @pl.when(pl.program_id(1) == 0)
    def _():
        state_ref[...] = jnp.zeros_like(state_ref)

    x = x_ref[0]
    xh = x.astype(BF16)

    def proj(off, width):
        return _dot(xh, w_ref[:, off:off + width]) + b_ref[:, off:off + width]

    v_sg = _layer_norm(_gelu(proj(OFF_V, D_MODEL)), sg_g_ref[...], sg_b_ref[...]).astype(BF16)
    row = jax.lax.broadcasted_iota(jnp.int32, (SG_CHUNK, SG_CHUNK), 0)
    col = jax.lax.broadcasted_iota(jnp.int32, (SG_CHUNK, SG_CHUNK), 1)
    w_s = [jnp.where(row >= col, ws_ref[g], 0.0).astype(BF16) for g in range(SG_GROUPS)]
    mixed_rows = []
    for c in range(MIX_BLOCK // SG_CHUNK):
        r0 = c * SG_CHUNK
        parts = [_dot(w_s[g], v_sg[r0:r0 + SG_CHUNK, g * LANES:(g + 1) * LANES]) for g in range(SG_GROUPS)]
        mixed_rows.append(jnp.concatenate(parts, axis=1) + bs_ref[...])
    mixed = jnp.concatenate(mixed_rows, axis=0)
    m_acc = _sigmoid(proj(OFF_GA, D_MODEL)) * (_gelu(proj(OFF_U, D_MODEL)) * mixed)

    q = proj(OFF_Q, GLA_KEY_DIM) * (GLA_HEAD_K ** -0.5)
    k = proj(OFF_K, GLA_KEY_DIM)
    gv = proj(OFF_GV, GLA_VAL_DIM).astype(BF16)
    a_low = _dot(xh, wlow_ref[...]) + blow_ref[...]
    a_hi, a_lo = _split_bf16(a_low)
    z = _dot(a_hi, wg_hi_ref[...]) + _dot(a_lo, wg_hi_ref[...]) + _dot(a_hi, wg_lo_ref[...]) + bg_ref[...]
    log_f = (jnp.minimum(z, 0.0) - jnp.log(1.0 + jnp.exp(-jnp.abs(z)))) * (1.0 / GLA_GATE_TEMP)

    lvl = lvl_ref[...]
    o_rows = []
    for c in range(MIX_BLOCK // GLA_CHUNK):
        r0 = c * GLA_CHUNK
        qc = q[r0:r0 + GLA_CHUNK]
        kc = k[r0:r0 + GLA_CHUNK]
        f_hi, f_lo = _split_bf16(log_f[r0:r0 + GLA_CHUNK])
        decay = jnp.exp(_dot(tab_ref[...], f_hi) + _dot(tab_ref[...], f_lo))

        def block(i):
            return decay[i * GLA_CHUNK:(i + 1) * GLA_CHUNK]

        scores = [jnp.zeros((GLA_CHUNK, GLA_CHUNK), F32) for _ in range(GLA_HEADS)]
        for lev in range(GLA_LEVELS + 1):
            if lev < GLA_LEVELS:
                qe = (qc * block(lev)).astype(BF16)
                ke = (kc * block(lev)).astype(BF16)
            else:
                qe = qc.astype(BF16)
                ke = kc.astype(BF16)
            for h in range(GLA_HEADS):
                sl = slice(h * GLA_HEAD_K, (h + 1) * GLA_HEAD_K)
                scores[h] = jnp.where(lvl == lev, _dot_nt(qe[:, sl], ke[:, sl]), scores[h])
        q_in = (qc * block(GLA_LEVELS)).astype(BF16)
        k_out = (kc * block(GLA_LEVELS + 1)).astype(BF16)
        total = block(GLA_LEVELS)[GLA_CHUNK - 1:GLA_CHUNK]
        heads = []
        for h in range(GLA_HEADS):
            sl = slice(h * GLA_HEAD_K, (h + 1) * GLA_HEAD_K)
            vh = gv[r0:r0 + GLA_CHUNK, h * GLA_HEAD_V:(h + 1) * GLA_HEAD_V]
            st = state_ref[h]
            o_h = _dot(scores[h].astype(BF16), vh) + _dot_nt(q_in[:, sl], st.astype(BF16))
            state_ref[h] = st * total[:, sl] + _dot_tn(vh, k_out[:, sl])
            o_h = o_h * jax.lax.rsqrt(jnp.mean(o_h * o_h, axis=-1, keepdims=True) + LN_EPS)
            heads.append(o_h)
        o_rows.append(jnp.concatenate(heads, axis=1))
    o = jnp.concatenate(o_rows, axis=0) * gn_ref[...]
    r = proj(OFF_R, GLA_VAL_DIM)
    y_b = o * (r * _sigmoid(r))
    m = m_acc + _sigmoid(proj(OFF_GB, D_MODEL)) * y_b

    h1 = alpha * x + _dot(m.astype(BF16), wout_ref[...])
    o_ref[0] = _layer_norm(h1, ln_g_ref[...], ln_b_ref[...])


def _ffn_kernel(h_ref, w1_ref, w2_ref, ln_g_ref, ln_b_ref, o_ref, *, alpha):
    h = h_ref[...]
    hb = h.astype(BF16)
    acc = alpha * h
    for j in range(D_FF // D_MODEL):
        a = jnp.maximum(_dot(hb, w1_ref[:, j * D_MODEL:(j + 1) * D_MODEL]), 0.0)
        acc = acc + _dot((a * a).astype(BF16), w2_ref[j * D_MODEL:(j + 1) * D_MODEL, :])
    o_ref[...] = _layer_norm(acc, ln_g_ref[...], ln_b_ref[...])


def _resident(shape):
    nd = len(shape)
    return pl.BlockSpec(shape, lambda *_: (0,) * nd, pipeline_mode=pl.Buffered(1))


def _mixer_call(x, w_main, b_main, w_low, b_low, wg_hi, wg_lo, bg, sg_g, sg_b, ws, bs_plane,
                table, level, gn, w_out, ln_g, ln_b, alpha):
    bsz, t, d = x.shape
    consts = (w_main, b_main, w_low, b_low, wg_hi, wg_lo, bg, sg_g, sg_b, ws, bs_plane,
              table, level, gn, w_out, ln_g, ln_b)
    return pl.pallas_call(
        functools.partial(_mixer_kernel, alpha=alpha),
        out_shape=jax.ShapeDtypeStruct((bsz, t, d), F32),
        grid=(bsz, t // MIX_BLOCK),
        in_specs=[pl.BlockSpec((1, MIX_BLOCK, d), lambda b, i: (b, i, 0))]
        + [_resident(c.shape) for c in consts],
        out_specs=pl.BlockSpec((1, MIX_BLOCK, d), lambda b, i: (b, i, 0)),
        scratch_shapes=[pltpu.VMEM((GLA_HEADS, GLA_HEAD_V, GLA_HEAD_K), F32)],
        compiler_params=pltpu.CompilerParams(
            dimension_semantics=("arbitrary", "arbitrary"), vmem_limit_bytes=VMEM_LIMIT),
        name="mixer",
    )(x, *consts)


def _ffn_call(h, w1, w2, ln_g, ln_b, alpha):
    n, d = h.shape
    return pl.pallas_call(
        functools.partial(_ffn_kernel, alpha=alpha),
        out_shape=jax.ShapeDtypeStruct((n, d), F32),
        grid=(n // FFN_BLOCK,),
        in_specs=[pl.BlockSpec((FFN_BLOCK, d), lambda i: (i, 0)),
                  _resident(w1.shape), _resident(w2.shape), _resident(ln_g.shape), _resident(ln_b.shape)],
        out_specs=pl.BlockSpec((FFN_BLOCK, d), lambda i: (i, 0)),
        compiler_params=pltpu.CompilerParams(
            dimension_semantics=("arbitrary",), vmem_limit_bytes=VMEM_LIMIT),
        name="ffn",
    )(h, w1, w2, ln_g, ln_b)


def kernel(x, w_in, b_in, sg_ln_g, sg_ln_b, sg_w_s, sg_b_s, gla_w_gate2, gla_b_gate, gla_norm_g,
           w_out, ln1_g, ln1_b, w_ff1, w_ff2, ln2_g, ln2_b):
    bsz, t, d = x.shape
    depth = w_in.shape[0]
    alpha = (2.0 * depth) ** 0.25
    table_np, level_np = _decay_tables()
    table = jnp.asarray(table_np, BF16)
    level = jnp.asarray(level_np)
    low0 = 2 * D_MODEL + 2 * GLA_KEY_DIM + 2 * GLA_VAL_DIM
    low1 = low0 + GLA_GATE_RANK
    pad = LANES - GLA_GATE_RANK
    h = x
    for l in range(depth):
        w_main = jnp.concatenate([w_in[l][:, :low0], w_in[l][:, low1:]], axis=1).astype(BF16)
        b_main = jnp.concatenate([b_in[l][:low0], b_in[l][low1:]])[None, :]
        w_low = jnp.pad(w_in[l][:, low0:low1], ((0, 0), (0, pad))).astype(BF16)
        b_low = jnp.pad(b_in[l][low0:low1], (0, pad))[None, :]
        wg = jnp.pad(gla_w_gate2[l], ((0, pad), (0, 0)))
        wg_hi = wg.astype(BF16)
        wg_lo = (wg - wg_hi.astype(F32)).astype(BF16)
        bs_plane = jnp.repeat(jnp.transpose(sg_b_s[l]), D_MODEL // SG_GROUPS, axis=1)
        h = _mixer_call(h, w_main, b_main, w_low, b_low, wg_hi, wg_lo, gla_b_gate[l][None, :],
                        sg_ln_g[l][None, :], sg_ln_b[l][None, :], sg_w_s[l], bs_plane, table, level,
                        gla_norm_g[l][None, :], w_out[l].astype(BF16), ln1_g[l][None, :], ln1_b[l][None, :],
                        alpha)
        h = _ffn_call(h.reshape(bsz * t, d), w_ff1[l].astype(BF16), w_ff2[l].astype(BF16),
                      ln2_g[l][None, :], ln2_b[l][None, :], alpha).reshape(bsz, t, d)
    return h
```

```python
import functools
import math

import jax
import jax.numpy as jnp
import numpy as np
from jax.experimental import pallas as pl
from jax.experimental.pallas import tpu as pltpu

D_MODEL = 1024
SG_CHUNK = 128
SG_GROUPS = 8
GLA_HEADS = 4
GLA_KEY_DIM = 512
GLA_VAL_DIM = 1024
GLA_HEAD_K = GLA_KEY_DIM // GLA_HEADS
GLA_HEAD_V = GLA_VAL_DIM // GLA_HEADS
GLA_GATE_RANK = 16
GLA_GATE_TEMP = 16.0
D_FF = 4 * D_MODEL
LN_EPS = 1e-5

LANES = 128
GLA_CHUNK = 128
GLA_LEVELS = int(math.log2(GLA_CHUNK))
MIX_BLOCK = 512
FFN_BLOCK = 512
VMEM_LIMIT = 56 * 1024 * 1024

OFF_U, OFF_V, OFF_Q, OFF_K, OFF_GV, OFF_R, OFF_GA, OFF_GB = 0, 1024, 2048, 2560, 3072, 4096, 5120, 6144
D_MAIN = 7168

GELU_C = 0.7978845608028654
LOG2E = 1.4426950408889634

BF16 = jnp.bfloat16
F32 = jnp.float32


def _decay_tables():
    c = GLA_CHUNK
    blocks = []
    for lw in range(GLA_LEVELS - 1, -1, -1):
        a = np.zeros((c, c), np.float32)
        for t in range(c):
            if (t >> lw) & 1:
                a[t, (t >> lw) << lw:t + 1] = 1.0
            else:
                a[t, t + 1:((t >> lw) + 1) << lw] = 1.0
        blocks.append(a)
    blocks.append(np.tril(np.ones((c, c), np.float32)))
    blocks.append(np.triu(np.ones((c, c), np.float32), k=1))
    table = np.concatenate(blocks, axis=0)
    t = np.arange(c)[:, None]
    s = np.arange(c)[None, :]
    x = t ^ s
    top = np.zeros_like(x)
    nz = x > 0
    top[nz] = np.floor(np.log2(x[nz])).astype(x.dtype)
    level = np.where(t > s, GLA_LEVELS - 1 - top, np.where(t == s, GLA_LEVELS, -1)).astype(np.int32)
    return table, level


def _layer_norm(x, g, b):
    mu = jnp.mean(x, axis=-1, keepdims=True)
    xc = x - mu
    var = jnp.mean(xc * xc, axis=-1, keepdims=True)
    return xc * jax.lax.rsqrt(var + LN_EPS) * g + b


def _gelu(x):
    hx = 0.5 * x
    return hx + hx * jnp.tanh(x * (GELU_C + (GELU_C * 0.044715) * (x * x)))


def _sigmoid(x):
    return 0.5 * jnp.tanh(0.5 * x) + 0.5


def _split_bf16(x):
    hi = x.astype(BF16)
    lo = (x - hi.astype(F32)).astype(BF16)
    return hi, lo


def _dot(a, b):
    return jnp.dot(a, b, preferred_element_type=F32)


def _dot_nt(a, b):
    return jax.lax.dot_general(a, b, (((1,), (1,)), ((), ())), preferred_element_type=F32)


def _dot_tn(a, b):
    return jax.lax.dot_general(a, b, (((0,), (0,)), ((), ())), preferred_element_type=F32)


def _mixer_kernel(x_ref, w_ref, b_ref, wlow_ref, blow_ref, wg_ref, bg_ref,
                  sg_g_ref, sg_b_ref, ws_ref, bs_ref, tab_ref, lvl_ref, upper_ref, gn_ref,
                  wout_ref, ln_g_ref, ln_b_ref, o_ref, state_ref, *, alpha):
    @pl.when(pl.program_id(1) == 0)
    def _():
        state_ref[...] = jnp.zeros_like(state_ref)

    x = x_ref[0]
    xh = x.astype(BF16)
    n_chunks = MIX_BLOCK // GLA_CHUNK

    def proj(off, width):
        return _dot(xh, w_ref[:, off:off + width]) + b_ref[:, off:off + width]

    v_sg = _layer_norm(_gelu(proj(OFF_V, D_MODEL)), sg_g_ref[...], sg_b_ref[...]).astype(BF16)
    row = jax.lax.broadcasted_iota(jnp.int32, (SG_CHUNK, SG_CHUNK), 0)
    col = jax.lax.broadcasted_iota(jnp.int32, (SG_CHUNK, SG_CHUNK), 1)
    mixed_g = []
    for g in range(SG_GROUPS):
        w_g = jnp.where(row >= col, ws_ref[g], 0.0).astype(BF16)
        gs = slice(g * LANES, (g + 1) * LANES)
        rhs = jnp.concatenate([v_sg[c * SG_CHUNK:(c + 1) * SG_CHUNK, gs] for c in range(MIX_BLOCK // SG_CHUNK)],
                              axis=1)
        mixed_g.append(_dot(w_g, rhs))
    mixed = jnp.concatenate(
        [jnp.concatenate([mg[:, c * LANES:(c + 1) * LANES] for mg in mixed_g], axis=1) + bs_ref[...]
         for c in range(MIX_BLOCK // SG_CHUNK)], axis=0)
    m_acc = _sigmoid(proj(OFF_GA, D_MODEL)) * (_gelu(proj(OFF_U, D_MODEL)) * mixed)

    q = proj(OFF_Q, GLA_KEY_DIM) * (GLA_HEAD_K ** -0.5)
    k = proj(OFF_K, GLA_KEY_DIM)
    gv = proj(OFF_GV, GLA_VAL_DIM).astype(BF16)
    a3 = _dot(xh, wlow_ref[...]) + blow_ref[...]
    a3_hi = a3.astype(BF16).astype(F32)
    lane = jax.lax.broadcasted_iota(jnp.int32, a3.shape, 1)
    mid = (lane >= GLA_GATE_RANK) & (lane < 2 * GLA_GATE_RANK)
    z = _dot(jnp.where(mid, a3 - a3_hi, a3_hi).astype(BF16), wg_ref[...]) + bg_ref[...]
    log_f = (jnp.minimum(z, 0.0) - jnp.log(1.0 + jnp.exp(-jnp.abs(z)))) * (LOG2E / GLA_GATE_TEMP)

    lvl2 = lvl_ref[...]
    klane = jax.lax.broadcasted_iota(jnp.int32, (GLA_CHUNK, GLA_KEY_DIM), 1)
    even_head = (klane // GLA_HEAD_K) % 2 == 0
    keep_even = jnp.where(even_head, 1.0, 0.0).astype(BF16)
    keep_odd = jnp.where(even_head, 0.0, 1.0).astype(BF16)
    pair_w = 2 * GLA_HEAD_K

    def pair_scores(lhs, rhs):
        bd = jnp.concatenate([rhs * keep_even, rhs * keep_odd], axis=0)
        return [_dot_nt(lhs[:, p * pair_w:(p + 1) * pair_w], bd[:, p * pair_w:(p + 1) * pair_w])
                for p in range(GLA_HEADS // 2)]

    o_rows = []
    for c in range(n_chunks):
        r0 = c * GLA_CHUNK
        qc = q[r0:r0 + GLA_CHUNK]
        kc = k[r0:r0 + GLA_CHUNK]
        f_hi, f_lo = _split_bf16(log_f[r0:r0 + GLA_CHUNK])
        decay = jnp.exp2(_dot(tab_ref[...], jnp.concatenate([f_hi, f_lo], axis=0)))

        def block(i):
            return decay[i * GLA_CHUNK:(i + 1) * GLA_CHUNK]

        scores = [jnp.zeros((GLA_CHUNK, 2 * GLA_CHUNK), F32) for _ in range(GLA_HEADS // 2)]
        for lev in range(GLA_LEVELS + 1):
            if lev < GLA_LEVELS:
                w = GLA_CHUNK >> (lev + 1)
                if w >= 8:
                    mixed_qk = jnp.concatenate(
                        [(qc if (b % 2) else kc)[b * w:(b + 1) * w] for b in range(GLA_CHUNK // w)], axis=0)
                else:
                    mixed_qk = jnp.where(upper_ref[GLA_LEVELS - 1 - lev] != 0, qc, kc)
                zl = (mixed_qk * block(lev)).astype(BF16)
                parts = pair_scores(zl, zl)
            else:
                parts = pair_scores(qc.astype(BF16), kc.astype(BF16))
            for p in range(GLA_HEADS // 2):
                scores[p] = jnp.where(lvl2 == lev, parts[p], scores[p])
        q_in = (qc * block(GLA_LEVELS)).astype(BF16)
        k_out = (kc * block(GLA_LEVELS + 1)).astype(BF16)
        total = block(GLA_LEVELS)[GLA_CHUNK - 1:GLA_CHUNK]
        heads = []
        for h in range(GLA_HEADS):
            sl = slice(h * GLA_HEAD_K, (h + 1) * GLA_HEAD_K)
            vh = gv[r0:r0 + GLA_CHUNK, h * GLA_HEAD_V:(h + 1) * GLA_HEAD_V]
            st = state_ref[h]
            s_h = scores[h // 2][:, (h % 2) * GLA_CHUNK:(h % 2 + 1) * GLA_CHUNK].astype(BF16)
            o_h = _dot(jnp.concatenate([s_h, q_in[:, sl]], axis=1),
                       jnp.concatenate([vh, st.astype(BF16)], axis=0))
            tcol = jnp.transpose(jnp.broadcast_to(total[:, sl], (GLA_HEAD_K, GLA_HEAD_K)))
            state_ref[h] = st * jnp.concatenate([tcol, tcol], axis=1) + _dot_tn(k_out[:, sl], vh)
            o_h = o_h * jax.lax.rsqrt(jnp.mean(o_h * o_h, axis=-1, keepdims=True) + LN_EPS)
            heads.append(o_h)
        o_rows.append(jnp.concatenate(heads, axis=1))
    o = jnp.concatenate(o_rows, axis=0) * gn_ref[...]
    r = proj(OFF_R, GLA_VAL_DIM)
    y_b = o * (r * _sigmoid(r))
    m = m_acc + _sigmoid(proj(OFF_GB, D_MODEL)) * y_b

    h1 = alpha * x + _dot(m.astype(BF16), wout_ref[...])
    o_ref[0] = _layer_norm(h1, ln_g_ref[...], ln_b_ref[...])


def _ffn_kernel(h_ref, w1_ref, w2_ref, ln_g_ref, ln_b_ref, o_ref, *, alpha):
    h = h_ref[...]
    hb = h.astype(BF16)
    acc = alpha * h
    for j in range(D_FF // D_MODEL):
        a = jnp.maximum(_dot(hb, w1_ref[:, j * D_MODEL:(j + 1) * D_MODEL]), 0.0)
        acc = acc + _dot((a * a).astype(BF16), w2_ref[j * D_MODEL:(j + 1) * D_MODEL, :])
    o_ref[...] = _layer_norm(acc, ln_g_ref[...], ln_b_ref[...])


def _resident(shape):
    nd = len(shape)
    return pl.BlockSpec(shape, lambda *_: (0,) * nd, pipeline_mode=pl.Buffered(1))


def _mixer_call(x, consts, alpha):
    bsz, t, d = x.shape
    return pl.pallas_call(
        functools.partial(_mixer_kernel, alpha=alpha),
        out_shape=jax.ShapeDtypeStruct((bsz, t, d), F32),
        grid=(bsz, t // MIX_BLOCK),
        in_specs=[pl.BlockSpec((1, MIX_BLOCK, d), lambda b, i: (b, i, 0))]
        + [_resident(c.shape) for c in consts],
        out_specs=pl.BlockSpec((1, MIX_BLOCK, d), lambda b, i: (b, i, 0)),
        scratch_shapes=[pltpu.VMEM((GLA_HEADS, GLA_HEAD_K, GLA_HEAD_V), F32)],
        compiler_params=pltpu.CompilerParams(
            dimension_semantics=("arbitrary", "arbitrary"), vmem_limit_bytes=VMEM_LIMIT),
        name="mixer",
    )(x, *consts)


def _ffn_call(h, w1, w2, ln_g, ln_b, alpha):
    n, d = h.shape
    return pl.pallas_call(
        functools.partial(_ffn_kernel, alpha=alpha),
        out_shape=jax.ShapeDtypeStruct((n, d), F32),
        grid=(n // FFN_BLOCK,),
        in_specs=[pl.BlockSpec((FFN_BLOCK, d), lambda i: (i, 0)),
                  _resident(w1.shape), _resident(w2.shape), _resident(ln_g.shape), _resident(ln_b.shape)],
        out_specs=pl.BlockSpec((FFN_BLOCK, d), lambda i: (i, 0)),
        compiler_params=pltpu.CompilerParams(
            dimension_semantics=("arbitrary",), vmem_limit_bytes=VMEM_LIMIT),
        name="ffn",
    )(h, w1, w2, ln_g, ln_b)


def kernel(x, w_in, b_in, sg_ln_g, sg_ln_b, sg_w_s, sg_b_s, gla_w_gate2, gla_b_gate, gla_norm_g,
           w_out, ln1_g, ln1_b, w_ff1, w_ff2, ln2_g, ln2_b):
    bsz, t, d = x.shape
    depth = w_in.shape[0]
    alpha = (2.0 * depth) ** 0.25
    table_np, level_np = _decay_tables()
    table = jnp.asarray(np.concatenate([table_np, table_np], axis=1), BF16)
    level = jnp.asarray(np.concatenate([level_np, level_np], axis=1))
    upper = jnp.asarray(np.stack([np.broadcast_to(((np.arange(GLA_CHUNK) >> j) & 1)[:, None],
                                                  (GLA_CHUNK, GLA_KEY_DIM)) for j in range(3)]).astype(np.int32))
    low0 = 2 * D_MODEL + 2 * GLA_KEY_DIM + 2 * GLA_VAL_DIM
    low1 = low0 + GLA_GATE_RANK
    rank = GLA_GATE_RANK
    h = x
    for l in range(depth):
        w_main = jnp.concatenate([w_in[l][:, :low0].astype(BF16), w_in[l][:, low1:].astype(BF16)], axis=1)
        b_main = jnp.concatenate([b_in[l][:low0], b_in[l][low1:]])[None, :]
        w_low = jnp.pad(jnp.tile(w_in[l][:, low0:low1].astype(BF16), (1, 3)), ((0, 0), (0, LANES - 3 * rank)))
        b_low = jnp.pad(jnp.tile(b_in[l][low0:low1], 3), (0, LANES - 3 * rank))[None, :]
        wg_hi = gla_w_gate2[l].astype(BF16)
        wg_lo = (gla_w_gate2[l] - wg_hi.astype(F32)).astype(BF16)
        wg = jnp.pad(jnp.concatenate([wg_hi, wg_hi, wg_lo], axis=0), ((0, LANES - 3 * rank), (0, 0)))
        bs_plane = jnp.repeat(jnp.transpose(sg_b_s[l]), D_MODEL // SG_GROUPS, axis=1)
        consts = (w_main, b_main, w_low, b_low, wg, gla_b_gate[l][None, :],
                  sg_ln_g[l][None, :], sg_ln_b[l][None, :], sg_w_s[l], bs_plane, table, level, upper,
                  gla_norm_g[l][None, :], w_out[l].astype(BF16), ln1_g[l][None, :], ln1_b[l][None, :])
        h = _mixer_call(h, consts, alpha)
        h = _ffn_call(h.reshape(bsz * t, d), w_ff1[l].astype(BF16), w_ff2[l].astype(BF16),
                      ln2_g[l][None, :], ln2_b[l][None, :], alpha).reshape(bsz, t, d)
    return h
```

```python
import functools
import math

import jax
import jax.numpy as jnp
import numpy as np
from jax.experimental import pallas as pl
from jax.experimental.pallas import tpu as pltpu

D_MODEL = 1024
SG_CHUNK = 128
SG_GROUPS = 8
GLA_HEADS = 4
GLA_KEY_DIM = 512
GLA_VAL_DIM = 1024
GLA_HEAD_K = GLA_KEY_DIM // GLA_HEADS
GLA_HEAD_V = GLA_VAL_DIM // GLA_HEADS
GLA_GATE_RANK = 16
GLA_GATE_TEMP = 16.0
D_FF = 4 * D_MODEL
LN_EPS = 1e-5

LANES = 128
GLA_CHUNK = 128
GLA_LEVELS = int(math.log2(GLA_CHUNK))
MIX_BLOCK = 512
FFN_BLOCK = 512
VMEM_LIMIT = 56 * 1024 * 1024

OFF_U, OFF_V, OFF_Q, OFF_K, OFF_GV, OFF_R, OFF_LOW, OFF_GA, OFF_GB = 0, 1024, 2048, 2560, 3072, 4096, 5120, 5136, 6160

GELU_C = 0.7978845608028654
LOG2E = 1.4426950408889634

BF16 = jnp.bfloat16
F32 = jnp.float32


def _decay_tables():
    c = GLA_CHUNK
    blocks = []
    for lw in range(GLA_LEVELS - 1, -1, -1):
        a = np.zeros((c, c), np.float32)
        for t in range(c):
            if (t >> lw) & 1:
                a[t, (t >> lw) << lw:t + 1] = 1.0
            else:
                a[t, t + 1:((t >> lw) + 1) << lw] = 1.0
        blocks.append(a)
    blocks.append(np.tril(np.ones((c, c), np.float32)))
    blocks.append(np.triu(np.ones((c, c), np.float32), k=1))
    table = np.concatenate(blocks, axis=0)
    t = np.arange(c)[:, None]
    s = np.arange(c)[None, :]
    x = t ^ s
    top = np.zeros_like(x)
    nz = x > 0
    top[nz] = np.floor(np.log2(x[nz])).astype(x.dtype)
    level = np.where(t > s, GLA_LEVELS - 1 - top, np.where(t == s, GLA_LEVELS, -1)).astype(np.int32)
    return table, level


def _layer_norm(x, g, b):
    mu = jnp.mean(x, axis=-1, keepdims=True)
    xc = x - mu
    var = jnp.mean(xc * xc, axis=-1, keepdims=True)
    return xc * jax.lax.rsqrt(var + LN_EPS) * g + b


def _gelu(x):
    hx = 0.5 * x
    return hx + hx * jnp.tanh(x * (GELU_C + (GELU_C * 0.044715) * (x * x)))


def _sigmoid(x):
    return 0.5 * jnp.tanh(0.5 * x) + 0.5


def _split_bf16(x):
    hi = x.astype(BF16)
    lo = (x - hi.astype(F32)).astype(BF16)
    return hi, lo


def _dot(a, b):
    return jnp.dot(a, b, preferred_element_type=F32)


def _dot_nt(a, b):
    return jax.lax.dot_general(a, b, (((1,), (1,)), ((), ())), preferred_element_type=F32)


def _dot_tn(a, b):
    return jax.lax.dot_general(a, b, (((0,), (0,)), ((), ())), preferred_element_type=F32)


def _mixer_kernel(x_ref, w_ref, b_ref, wlow_ref, blow_ref, wg_ref, bg_ref,
                  sg_g_ref, sg_b_ref, ws_ref, bs_ref, tab_ref, lvl_ref, upper_ref, gn_ref,
                  wout_ref, ln_g_ref, ln_b_ref, o_ref, state_ref, *, alpha):
    @pl.when(pl.program_id(1) == 0)
    def _():
        state_ref[...] = jnp.zeros_like(state_ref)

    x = x_ref[0]
    xh = x.astype(BF16)
    n_chunks = MIX_BLOCK // GLA_CHUNK

    def proj(off, width):
        b_off = off if off < OFF_LOW else off - GLA_GATE_RANK
        return _dot_nt(xh, w_ref[off:off + width, :]) + b_ref[:, b_off:b_off + width]

    v_sg = _layer_norm(_gelu(proj(OFF_V, D_MODEL)), sg_g_ref[...], sg_b_ref[...]).astype(BF16)
    row = jax.lax.broadcasted_iota(jnp.int32, (SG_CHUNK, SG_CHUNK), 0)
    col = jax.lax.broadcasted_iota(jnp.int32, (SG_CHUNK, SG_CHUNK), 1)
    mixed_g = []
    for g in range(SG_GROUPS):
        w_g = jnp.where(row >= col, ws_ref[g], 0.0).astype(BF16)
        gs = slice(g * LANES, (g + 1) * LANES)
        rhs = jnp.concatenate([v_sg[c * SG_CHUNK:(c + 1) * SG_CHUNK, gs] for c in range(MIX_BLOCK // SG_CHUNK)],
                              axis=1)
        mixed_g.append(_dot(w_g, rhs))
    mixed = jnp.concatenate(
        [jnp.concatenate([mg[:, c * LANES:(c + 1) * LANES] for mg in mixed_g], axis=1) + bs_ref[...]
         for c in range(MIX_BLOCK // SG_CHUNK)], axis=0)
    m_acc = _sigmoid(proj(OFF_GA, D_MODEL)) * (_gelu(proj(OFF_U, D_MODEL)) * mixed)

    q = proj(OFF_Q, GLA_KEY_DIM) * (GLA_HEAD_K ** -0.5)
    k = proj(OFF_K, GLA_KEY_DIM)
    gv = proj(OFF_GV, GLA_VAL_DIM).astype(BF16)
    a3 = _dot_nt(xh, wlow_ref[...]) + blow_ref[...]
    a3_hi = a3.astype(BF16).astype(F32)
    lane = jax.lax.broadcasted_iota(jnp.int32, a3.shape, 1)
    mid = (lane >= GLA_GATE_RANK) & (lane < 2 * GLA_GATE_RANK)
    z = _dot(jnp.where(mid, a3 - a3_hi, a3_hi).astype(BF16), wg_ref[...]) + bg_ref[...]
    log_f = (jnp.minimum(z, 0.0) - jnp.log(1.0 + jnp.exp(-jnp.abs(z)))) * (LOG2E / GLA_GATE_TEMP)

    lvl2 = lvl_ref[...]
    klane = jax.lax.broadcasted_iota(jnp.int32, (GLA_CHUNK, GLA_KEY_DIM), 1)
    even_head = (klane // GLA_HEAD_K) % 2 == 0
    keep_even = jnp.where(even_head, 1.0, 0.0).astype(BF16)
    keep_odd = jnp.where(even_head, 0.0, 1.0).astype(BF16)
    pair_w = 2 * GLA_HEAD_K

    def pair_scores(lhs, rhs):
        bd = jnp.concatenate([rhs * keep_even, rhs * keep_odd], axis=0)
        return [_dot_nt(lhs[:, p * pair_w:(p + 1) * pair_w], bd[:, p * pair_w:(p + 1) * pair_w])
                for p in range(GLA_HEADS // 2)]

    o_rows = []
    for c in range(n_chunks):
        r0 = c * GLA_CHUNK
        qc = q[r0:r0 + GLA_CHUNK]
        kc = k[r0:r0 + GLA_CHUNK]
        f_hi, f_lo = _split_bf16(log_f[r0:r0 + GLA_CHUNK])
        decay = jnp.exp2(_dot(tab_ref[...], jnp.concatenate([f_hi, f_lo], axis=0)))

        def block(i):
            return decay[i * GLA_CHUNK:(i + 1) * GLA_CHUNK]

        scores = [jnp.zeros((GLA_CHUNK, 2 * GLA_CHUNK), F32) for _ in range(GLA_HEADS // 2)]
        for lev in range(GLA_LEVELS + 1):
            if lev < GLA_LEVELS:
                w = GLA_CHUNK >> (lev + 1)
                if w >= 8:
                    mixed_qk = jnp.concatenate(
                        [(qc if (b % 2) else kc)[b * w:(b + 1) * w] for b in range(GLA_CHUNK // w)], axis=0)
                else:
                    mixed_qk = jnp.where(upper_ref[GLA_LEVELS - 1 - lev] != 0, qc, kc)
                zl = (mixed_qk * block(lev)).astype(BF16)
                parts = pair_scores(zl, zl)
            else:
                parts = pair_scores(qc.astype(BF16), kc.astype(BF16))
            for p in range(GLA_HEADS // 2):
                scores[p] = jnp.where(lvl2 == lev, parts[p], scores[p])
        q_in = (qc * block(GLA_LEVELS)).astype(BF16)
        k_out = (kc * block(GLA_LEVELS + 1)).astype(BF16)
        total = block(GLA_LEVELS)[GLA_CHUNK - 1:GLA_CHUNK]
        heads = []
        for h in range(GLA_HEADS):
            sl = slice(h * GLA_HEAD_K, (h + 1) * GLA_HEAD_K)
            vh = gv[r0:r0 + GLA_CHUNK, h * GLA_HEAD_V:(h + 1) * GLA_HEAD_V]
            st = state_ref[h]
            s_h = scores[h // 2][:, (h % 2) * GLA_CHUNK:(h % 2 + 1) * GLA_CHUNK].astype(BF16)
            o_h = _dot(jnp.concatenate([s_h, q_in[:, sl]], axis=1),
                       jnp.concatenate([vh, st.astype(BF16)], axis=0))
            tcol = jnp.transpose(jnp.broadcast_to(total[:, sl], (GLA_HEAD_K, GLA_HEAD_K)))
            state_ref[h] = st * jnp.concatenate([tcol, tcol], axis=1) + _dot_tn(k_out[:, sl], vh)
            o_h = o_h * jax.lax.rsqrt(jnp.mean(o_h * o_h, axis=-1, keepdims=True) + LN_EPS)
            heads.append(o_h)
        o_rows.append(jnp.concatenate(heads, axis=1))
    o = jnp.concatenate(o_rows, axis=0) * gn_ref[...]
    r = proj(OFF_R, GLA_VAL_DIM)
    y_b = o * (r * _sigmoid(r))
    m = m_acc + _sigmoid(proj(OFF_GB, D_MODEL)) * y_b

    h1 = alpha * x + _dot(m.astype(BF16), wout_ref[...])
    o_ref[0] = _layer_norm(h1, ln_g_ref[...], ln_b_ref[...])


def _ffn_kernel(h_ref, w1_ref, w2_ref, ln_g_ref, ln_b_ref, o_ref, *, alpha):
    h = h_ref[...]
    hb = h.astype(BF16)
    acc = alpha * h
    for j in range(D_FF // D_MODEL):
        a = jnp.maximum(_dot(hb, w1_ref[:, j * D_MODEL:(j + 1) * D_MODEL]), 0.0)
        acc = acc + _dot((a * a).astype(BF16), w2_ref[j * D_MODEL:(j + 1) * D_MODEL, :])
    o_ref[...] = _layer_norm(acc, ln_g_ref[...], ln_b_ref[...])


def _resident(shape):
    nd = len(shape)
    return pl.BlockSpec(shape, lambda *_: (0,) * nd, pipeline_mode=pl.Buffered(1))


def _mixer_call(x, consts, alpha):
    bsz, t, d = x.shape
    return pl.pallas_call(
        functools.partial(_mixer_kernel, alpha=alpha),
        out_shape=jax.ShapeDtypeStruct((bsz, t, d), F32),
        grid=(bsz, t // MIX_BLOCK),
        in_specs=[pl.BlockSpec((1, MIX_BLOCK, d), lambda b, i: (b, i, 0))]
        + [_resident(c.shape) for c in consts],
        out_specs=pl.BlockSpec((1, MIX_BLOCK, d), lambda b, i: (b, i, 0)),
        scratch_shapes=[pltpu.VMEM((GLA_HEADS, GLA_HEAD_K, GLA_HEAD_V), F32)],
        compiler_params=pltpu.CompilerParams(
            dimension_semantics=("arbitrary", "arbitrary"), vmem_limit_bytes=VMEM_LIMIT),
        name="mixer",
    )(x, *consts)


def _ffn_call(h, w1, w2, ln_g, ln_b, alpha):
    n, d = h.shape
    return pl.pallas_call(
        functools.partial(_ffn_kernel, alpha=alpha),
        out_shape=jax.ShapeDtypeStruct((n, d), F32),
        grid=(n // FFN_BLOCK,),
        in_specs=[pl.BlockSpec((FFN_BLOCK, d), lambda i: (i, 0)),
                  _resident(w1.shape), _resident(w2.shape), _resident(ln_g.shape), _resident(ln_b.shape)],
        out_specs=pl.BlockSpec((FFN_BLOCK, d), lambda i: (i, 0)),
        compiler_params=pltpu.CompilerParams(
            dimension_semantics=("arbitrary",), vmem_limit_bytes=VMEM_LIMIT),
        name="ffn",
    )(h, w1, w2, ln_g, ln_b)


def kernel(x, w_in, b_in, sg_ln_g, sg_ln_b, sg_w_s, sg_b_s, gla_w_gate2, gla_b_gate, gla_norm_g,
           w_out, ln1_g, ln1_b, w_ff1, w_ff2, ln2_g, ln2_b):
    bsz, t, d = x.shape
    depth = w_in.shape[0]
    alpha = (2.0 * depth) ** 0.25
    table_np, level_np = _decay_tables()
    table = jnp.asarray(np.concatenate([table_np, table_np], axis=1), BF16)
    level = jnp.asarray(np.concatenate([level_np, level_np], axis=1))
    upper = jnp.asarray(np.stack([np.broadcast_to(((np.arange(GLA_CHUNK) >> j) & 1)[:, None],
                                                  (GLA_CHUNK, GLA_KEY_DIM)) for j in range(3)]).astype(np.int32))
    low0, low1 = OFF_LOW, OFF_LOW + GLA_GATE_RANK
    rank = GLA_GATE_RANK
    h = x
    for l in range(depth):
        w_main = jnp.transpose(w_in[l]).astype(BF16)
        b_main = jnp.concatenate([b_in[l][:low0], b_in[l][low1:]])[None, :]
        w_low = jnp.pad(jnp.tile(w_main[low0:low1], (3, 1)), ((0, LANES - 3 * rank), (0, 0)))
        b_low = jnp.pad(jnp.tile(b_in[l][low0:low1], 3), (0, LANES - 3 * rank))[None, :]
        wg_hi = gla_w_gate2[l].astype(BF16)
        wg_lo = (gla_w_gate2[l] - wg_hi.astype(F32)).astype(BF16)
        wg = jnp.pad(jnp.concatenate([wg_hi, wg_hi, wg_lo], axis=0), ((0, LANES - 3 * rank), (0, 0)))
        bs_plane = jnp.repeat(jnp.transpose(sg_b_s[l]), D_MODEL // SG_GROUPS, axis=1)
        consts = (w_main, b_main, w_low, b_low, wg, gla_b_gate[l][None, :],
                  sg_ln_g[l][None, :], sg_ln_b[l][None, :], sg_w_s[l], bs_plane, table, level, upper,
                  gla_norm_g[l][None, :], w_out[l].astype(BF16), ln1_g[l][None, :], ln1_b[l][None, :])
        h = _mixer_call(h, consts, alpha)
        h = _ffn_call(h.reshape(bsz * t, d), w_ff1[l].astype(BF16), w_ff2[l].astype(BF16),
                      ln2_g[l][None, :], ln2_b[l][None, :], alpha).reshape(bsz, t, d)
    return h
```

```python
import functools
import math

import jax
import jax.numpy as jnp
import numpy as np
from jax.experimental import pallas as pl
from jax.experimental.pallas import tpu as pltpu

D_MODEL = 1024
SG_CHUNK = 128
SG_GROUPS = 8
GLA_HEADS = 4
GLA_KEY_DIM = 512
GLA_VAL_DIM = 1024
GLA_HEAD_K = GLA_KEY_DIM // GLA_HEADS
GLA_HEAD_V = GLA_VAL_DIM // GLA_HEADS
GLA_GATE_RANK = 16
GLA_GATE_TEMP = 16.0
D_FF = 4 * D_MODEL
LN_EPS = 1e-5

LANES = 128
GLA_CHUNK = 128
GLA_LEVELS = int(math.log2(GLA_CHUNK))
BLOCK = 256
VMEM_LIMIT = 56 * 1024 * 1024

OFF_U, OFF_V, OFF_Q, OFF_K, OFF_GV, OFF_R, OFF_LOW, OFF_GA, OFF_GB = 0, 1024, 2048, 2560, 3072, 4096, 5120, 5136, 6160

GELU_C = 0.7978845608028654
LOG2E = 1.4426950408889634

BF16 = jnp.bfloat16
F32 = jnp.float32


def _decay_tables():
    c = GLA_CHUNK
    blocks = []
    for lw in range(GLA_LEVELS - 1, -1, -1):
        a = np.zeros((c, c), np.float32)
        for t in range(c):
            if (t >> lw) & 1:
                a[t, (t >> lw) << lw:t + 1] = 1.0
            else:
                a[t, t + 1:((t >> lw) + 1) << lw] = 1.0
        blocks.append(a)
    blocks.append(np.tril(np.ones((c, c), np.float32)))
    blocks.append(np.triu(np.ones((c, c), np.float32), k=1))
    table = np.concatenate(blocks, axis=0)
    t = np.arange(c)[:, None]
    s = np.arange(c)[None, :]
    x = t ^ s
    top = np.zeros_like(x)
    nz = x > 0
    top[nz] = np.floor(np.log2(x[nz])).astype(x.dtype)
    level = np.where(t > s, GLA_LEVELS - 1 - top, np.where(t == s, GLA_LEVELS, -1)).astype(np.int32)
    return table, level


def _layer_norm(x, g, b):
    mu = jnp.mean(x, axis=-1, keepdims=True)
    xc = x - mu
    var = jnp.mean(xc * xc, axis=-1, keepdims=True)
    return xc * jax.lax.rsqrt(var + LN_EPS) * g + b


def _gelu(x):
    hx = 0.5 * x
    return hx + hx * jnp.tanh(x * (GELU_C + (GELU_C * 0.044715) * (x * x)))


def _sigmoid(x):
    return 0.5 * jnp.tanh(0.5 * x) + 0.5


def _split_bf16(x):
    hi = x.astype(BF16)
    lo = (x - hi.astype(F32)).astype(BF16)
    return hi, lo


def _dot(a, b):
    return jnp.dot(a, b, preferred_element_type=F32)


def _dot_nt(a, b):
    return jax.lax.dot_general(a, b, (((1,), (1,)), ((), ())), preferred_element_type=F32)


def _dot_tn(a, b):
    return jax.lax.dot_general(a, b, (((0,), (0,)), ((), ())), preferred_element_type=F32)


def _layer_kernel(x_ref, w_ref, b_ref, wlow_ref, blow_ref, wg_ref, bg_ref,
                  sg_g_ref, sg_b_ref, ws_ref, bs_ref, tab_ref, lvl_ref, upper_ref, gn_ref,
                  wout_ref, ln_g_ref, ln_b_ref, w1_ref, w2_ref, ln2_g_ref, ln2_b_ref,
                  o_ref, state_ref, h_ref, *, alpha, blocks_per_row):
    step = pl.program_id(0)

    @pl.when(step % blocks_per_row == 0)
    def _():
        state_ref[...] = jnp.zeros_like(state_ref)

    @pl.when(step == 0)
    def _():
        h_ref[...] = jnp.zeros_like(h_ref)

    x = x_ref[0]
    xh = x.astype(BF16)
    h_prev = h_ref[...]
    hb = h_prev.astype(BF16)
    n_chunks = BLOCK // GLA_CHUNK

    def ffn_up(j):
        a = jnp.maximum(_dot(hb, w1_ref[:, j * D_MODEL:(j + 1) * D_MODEL]), 0.0)
        return (a * a).astype(BF16)

    def ffn_down(a, j):
        return _dot(a, w2_ref[j * D_MODEL:(j + 1) * D_MODEL, :])

    def proj(off, width):
        b_off = off if off < OFF_LOW else off - GLA_GATE_RANK
        return _dot_nt(xh, w_ref[off:off + width, :]) + b_ref[:, b_off:b_off + width]

    pv = proj(OFF_V, D_MODEL)
    q = proj(OFF_Q, GLA_KEY_DIM) * (GLA_HEAD_K ** -0.5)
    k = proj(OFF_K, GLA_KEY_DIM)
    a3 = _dot_nt(xh, wlow_ref[...]) + blow_ref[...]
    f_act = ffn_up(0)

    v_sg = _layer_norm(_gelu(pv), sg_g_ref[...], sg_b_ref[...]).astype(BF16)
    a3_hi = a3.astype(BF16).astype(F32)
    lane = jax.lax.broadcasted_iota(jnp.int32, a3.shape, 1)
    mid = (lane >= GLA_GATE_RANK) & (lane < 2 * GLA_GATE_RANK)
    z = _dot(jnp.where(mid, a3 - a3_hi, a3_hi).astype(BF16), wg_ref[...]) + bg_ref[...]
    pu = proj(OFF_U, D_MODEL)
    f_acc = alpha * h_prev + ffn_down(f_act, 0)
    log_f = (jnp.minimum(z, 0.0) - jnp.log(1.0 + jnp.exp(-jnp.abs(z)))) * (LOG2E / GLA_GATE_TEMP)

    row = jax.lax.broadcasted_iota(jnp.int32, (SG_CHUNK, SG_CHUNK), 0)
    col = jax.lax.broadcasted_iota(jnp.int32, (SG_CHUNK, SG_CHUNK), 1)
    mixed_g = []
    for g in range(SG_GROUPS):
        w_g = jnp.where(row >= col, ws_ref[g], 0.0).astype(BF16)
        gs = slice(g * LANES, (g + 1) * LANES)
        rhs = jnp.concatenate([v_sg[c * SG_CHUNK:(c + 1) * SG_CHUNK, gs] for c in range(BLOCK // SG_CHUNK)],
                              axis=1)
        mixed_g.append(_dot(w_g, rhs))
    mixed = jnp.concatenate(
        [jnp.concatenate([mg[:, c * LANES:(c + 1) * LANES] for mg in mixed_g], axis=1) + bs_ref[...]
         for c in range(BLOCK // SG_CHUNK)], axis=0)
    pga = proj(OFF_GA, D_MODEL)
    f_act = ffn_up(1)

    decays = []
    for c in range(n_chunks):
        f_hi, f_lo = _split_bf16(log_f[c * GLA_CHUNK:(c + 1) * GLA_CHUNK])
        decays.append(jnp.exp2(_dot(tab_ref[...], jnp.concatenate([f_hi, f_lo], axis=0))))
    m_acc = _sigmoid(pga) * (_gelu(pu) * mixed)
    gv = proj(OFF_GV, GLA_VAL_DIM).astype(BF16)
    f_acc = f_acc + ffn_down(f_act, 1)

    lvl2 = lvl_ref[...]
    klane = jax.lax.broadcasted_iota(jnp.int32, (GLA_CHUNK, GLA_KEY_DIM), 1)
    even_head = (klane // GLA_HEAD_K) % 2 == 0
    keep_even = jnp.where(even_head, 1.0, 0.0).astype(BF16)
    keep_odd = jnp.where(even_head, 0.0, 1.0).astype(BF16)
    pair_w = 2 * GLA_HEAD_K

    def pair_scores(lhs, rhs):
        bd = jnp.concatenate([rhs * keep_even, rhs * keep_odd], axis=0)
        return [_dot_nt(lhs[:, p * pair_w:(p + 1) * pair_w], bd[:, p * pair_w:(p + 1) * pair_w])
                for p in range(GLA_HEADS // 2)]

    def chunk_scores(c):
        qc = q[c * GLA_CHUNK:(c + 1) * GLA_CHUNK]
        kc = k[c * GLA_CHUNK:(c + 1) * GLA_CHUNK]
        scores = [jnp.zeros((GLA_CHUNK, 2 * GLA_CHUNK), F32) for _ in range(GLA_HEADS // 2)]
        for lev in range(GLA_LEVELS + 1):
            if lev < GLA_LEVELS:
                w = GLA_CHUNK >> (lev + 1)
                if w >= 8:
                    mixed_qk = jnp.concatenate(
                        [(qc if (b % 2) else kc)[b * w:(b + 1) * w] for b in range(GLA_CHUNK // w)], axis=0)
                else:
                    mixed_qk = jnp.where(upper_ref[GLA_LEVELS - 1 - lev] != 0, qc, kc)
                zl = (mixed_qk * decays[c][lev * GLA_CHUNK:(lev + 1) * GLA_CHUNK]).astype(BF16)
                parts = pair_scores(zl, zl)
            else:
                parts = pair_scores(qc.astype(BF16), kc.astype(BF16))
            for p in range(GLA_HEADS // 2):
                scores[p] = jnp.where(lvl2 == lev, parts[p], scores[p])
        return scores

    all_scores = [chunk_scores(c) for c in range(n_chunks)]
    r = proj(OFF_R, GLA_VAL_DIM)
    f_act = ffn_up(2)

    o_rows = []
    for c in range(n_chunks):
        r0 = c * GLA_CHUNK
        prefix = decays[c][GLA_LEVELS * GLA_CHUNK:(GLA_LEVELS + 1) * GLA_CHUNK]
        suffix = decays[c][(GLA_LEVELS + 1) * GLA_CHUNK:(GLA_LEVELS + 2) * GLA_CHUNK]
        q_in = (q[r0:r0 + GLA_CHUNK] * prefix).astype(BF16)
        k_out = (k[r0:r0 + GLA_CHUNK] * suffix).astype(BF16)
        total = prefix[GLA_CHUNK - 1:GLA_CHUNK]
        heads = []
        for h in range(GLA_HEADS):
            sl = slice(h * GLA_HEAD_K, (h + 1) * GLA_HEAD_K)
            vh = gv[r0:r0 + GLA_CHUNK, h * GLA_HEAD_V:(h + 1) * GLA_HEAD_V]
            st = state_ref[h]
            s_h = all_scores[c][h // 2][:, (h % 2) * GLA_CHUNK:(h % 2 + 1) * GLA_CHUNK].astype(BF16)
            o_h = _dot(jnp.concatenate([s_h, q_in[:, sl]], axis=1),
                       jnp.concatenate([vh, st.astype(BF16)], axis=0))
            tcol = jnp.transpose(jnp.broadcast_to(total[:, sl], (GLA_HEAD_K, GLA_HEAD_K)))
            state_ref[h] = st * jnp.concatenate([tcol, tcol], axis=1) + _dot_tn(k_out[:, sl], vh)
            o_h = o_h * jax.lax.rsqrt(jnp.mean(o_h * o_h, axis=-1, keepdims=True) + LN_EPS)
            heads.append(o_h)
        o_rows.append(jnp.concatenate(heads, axis=1))
    f_acc = f_acc + ffn_down(f_act, 2)
    pgb = proj(OFF_GB, D_MODEL)
    f_act = ffn_up(3)
    y_b = (jnp.concatenate(o_rows, axis=0) * gn_ref[...]) * (r * _sigmoid(r))
    m = m_acc + _sigmoid(pgb) * y_b

    h1 = alpha * x + _dot(m.astype(BF16), wout_ref[...])
    f_acc = f_acc + ffn_down(f_act, 3)
    h_ref[...] = _layer_norm(h1, ln_g_ref[...], ln_b_ref[...])
    o_ref[0] = _layer_norm(f_acc, ln2_g_ref[...], ln2_b_ref[...])


def _resident(shape):
    nd = len(shape)
    return pl.BlockSpec(shape, lambda *_: (0,) * nd, pipeline_mode=pl.Buffered(1))


def _layer_call(x, consts, alpha):
    bsz, t, d = x.shape
    blocks_per_row = t // BLOCK
    n_blocks = bsz * blocks_per_row

    def in_block(j):
        jj = jnp.minimum(j, n_blocks - 1)
        return (jj // blocks_per_row, jj % blocks_per_row, 0)

    def out_block(j):
        jj = jnp.maximum(j - 1, 0)
        return (jj // blocks_per_row, jj % blocks_per_row, 0)

    return pl.pallas_call(
        functools.partial(_layer_kernel, alpha=alpha, blocks_per_row=blocks_per_row),
        out_shape=jax.ShapeDtypeStruct((bsz, t, d), F32),
        grid=(n_blocks + 1,),
        in_specs=[pl.BlockSpec((1, BLOCK, d), in_block)] + [_resident(c.shape) for c in consts],
        out_specs=pl.BlockSpec((1, BLOCK, d), out_block),
        scratch_shapes=[pltpu.VMEM((GLA_HEADS, GLA_HEAD_K, GLA_HEAD_V), F32),
                        pltpu.VMEM((BLOCK, d), F32)],
        compiler_params=pltpu.CompilerParams(
            dimension_semantics=("arbitrary",), vmem_limit_bytes=VMEM_LIMIT),
        name="layer",
    )(x, *consts)


def kernel(x, w_in, b_in, sg_ln_g, sg_ln_b, sg_w_s, sg_b_s, gla_w_gate2, gla_b_gate, gla_norm_g,
           w_out, ln1_g, ln1_b, w_ff1, w_ff2, ln2_g, ln2_b):
    bsz, t, d = x.shape
    depth = w_in.shape[0]
    alpha = (2.0 * depth) ** 0.25
    table_np, level_np = _decay_tables()
    table = jnp.asarray(np.concatenate([table_np, table_np], axis=1), BF16)
    level = jnp.asarray(np.concatenate([level_np, level_np], axis=1))
    upper = jnp.asarray(np.stack([np.broadcast_to(((np.arange(GLA_CHUNK) >> j) & 1)[:, None],
                                                  (GLA_CHUNK, GLA_KEY_DIM)) for j in range(3)]).astype(np.int32))
    low0, low1 = OFF_LOW, OFF_LOW + GLA_GATE_RANK
    rank = GLA_GATE_RANK
    h = x
    for l in range(depth):
        w_main = jnp.transpose(w_in[l]).astype(BF16)
        b_main = jnp.concatenate([b_in[l][:low0], b_in[l][low1:]])[None, :]
        w_low = jnp.pad(jnp.tile(w_main[low0:low1], (3, 1)), ((0, LANES - 3 * rank), (0, 0)))
        b_low = jnp.pad(jnp.tile(b_in[l][low0:low1], 3), (0, LANES - 3 * rank))[None, :]
        wg_hi = gla_w_gate2[l].astype(BF16)
        wg_lo = (gla_w_gate2[l] - wg_hi.astype(F32)).astype(BF16)
        wg = jnp.pad(jnp.concatenate([wg_hi, wg_hi, wg_lo], axis=0), ((0, LANES - 3 * rank), (0, 0)))
        bs_plane = jnp.repeat(jnp.transpose(sg_b_s[l]), D_MODEL // SG_GROUPS, axis=1)
        consts = (w_main, b_main, w_low, b_low, wg, gla_b_gate[l][None, :],
                  sg_ln_g[l][None, :], sg_ln_b[l][None, :], sg_w_s[l], bs_plane, table, level, upper,
                  gla_norm_g[l][None, :], w_out[l].astype(BF16), ln1_g[l][None, :], ln1_b[l][None, :],
                  w_ff1[l].astype(BF16), w_ff2[l].astype(BF16), ln2_g[l][None, :], ln2_b[l][None, :])
        h = _layer_call(h, consts, alpha)
    return h
```

```python
import functools
import math

import jax
import jax.numpy as jnp
import numpy as np
from jax.experimental import pallas as pl
from jax.experimental.pallas import tpu as pltpu

D_MODEL = 1024
SG_CHUNK = 128
SG_GROUPS = 8
GLA_HEADS = 4
GLA_KEY_DIM = 512
GLA_VAL_DIM = 1024
GLA_HEAD_K = GLA_KEY_DIM // GLA_HEADS
GLA_HEAD_V = GLA_VAL_DIM // GLA_HEADS
GLA_GATE_RANK = 16
GLA_GATE_TEMP = 16.0
D_FF = 4 * D_MODEL
LN_EPS = 1e-5

LANES = 128
BF16_ROWS = 16
GLA_CHUNK = 128
GLA_LEVELS = int(math.log2(GLA_CHUNK))
MIX_BLOCK = 512
FFN_BLOCK = 1024
VMEM_LIMIT = 56 * 1024 * 1024

OFF_U, OFF_V, OFF_Q, OFF_K, OFF_GV, OFF_R, OFF_LOW, OFF_GA, OFF_GB = 0, 1024, 2048, 2560, 3072, 4096, 5120, 5136, 6160

GELU_C = 0.7978845608028654
LOG2E = 1.4426950408889634

BF16 = jnp.bfloat16
F32 = jnp.float32


def _decay_tables():
    c = GLA_CHUNK
    blocks = []
    for lw in range(GLA_LEVELS - 1, -1, -1):
        a = np.zeros((c, c), np.float32)
        for t in range(c):
            if (t >> lw) & 1:
                a[t, (t >> lw) << lw:t + 1] = 1.0
            else:
                a[t, t + 1:((t >> lw) + 1) << lw] = 1.0
        blocks.append(a)
    blocks.append(np.tril(np.ones((c, c), np.float32)))
    table = np.concatenate(blocks, axis=0)
    t = np.arange(c)[:, None]
    s = np.arange(c)[None, :]
    x = t ^ s
    top = np.zeros_like(x)
    nz = x > 0
    top[nz] = np.floor(np.log2(x[nz])).astype(x.dtype)
    level = np.where(t > s, GLA_LEVELS - 1 - top, np.where(t == s, GLA_LEVELS, -1)).astype(np.int32)
    return table, level


def _layer_norm(x, g, b):
    mu = jnp.mean(x, axis=-1, keepdims=True)
    xc = x - mu
    var = jnp.mean(xc * xc, axis=-1, keepdims=True)
    return xc * jax.lax.rsqrt(var + LN_EPS) * g + b


def _gelu(x):
    hx = 0.5 * x
    return hx + hx * jnp.tanh(x * (GELU_C + (GELU_C * 0.044715) * (x * x)))


def _sigmoid(x):
    return 0.5 * jnp.tanh(0.5 * x) + 0.5


def _split_bf16(x):
    hi = x.astype(BF16)
    lo = (x - hi.astype(F32)).astype(BF16)
    return hi, lo


def _dot(a, b):
    return jnp.dot(a, b, preferred_element_type=F32)


def _dot_nt(a, b):
    return jax.lax.dot_general(a, b, (((1,), (1,)), ((), ())), preferred_element_type=F32)


def _dot_tn(a, b):
    return jax.lax.dot_general(a, b, (((0,), (0,)), ((), ())), preferred_element_type=F32)


def _mixer_kernel(x_ref, w_ref, b_ref, wlow_ref, blow_ref, wg_ref, bg_ref,
                  sg_g_ref, sg_b_ref, ws_ref, bs_ref, tab_ref, lvl_ref, upper_ref, gn_ref,
                  wout_ref, ln_g_ref, ln_b_ref, o_ref, state_ref, *, alpha):
    @pl.when(pl.program_id(1) == 0)
    def _():
        state_ref[...] = jnp.zeros_like(state_ref)

    x = x_ref[0]
    xh = x.astype(BF16)
    n_chunks = MIX_BLOCK // GLA_CHUNK

    def proj(off, width):
        b_off = off if off < OFF_LOW else off - GLA_GATE_RANK
        return _dot_nt(xh, w_ref[off:off + width, :]) + b_ref[:, b_off:b_off + width]

    pv = proj(OFF_V, D_MODEL)
    q = proj(OFF_Q, GLA_KEY_DIM) * (GLA_HEAD_K ** -0.5)
    k = proj(OFF_K, GLA_KEY_DIM)
    a3 = _dot_nt(xh, wlow_ref[...]) + blow_ref[...]

    v_sg = _layer_norm(_gelu(pv), sg_g_ref[...], sg_b_ref[...]).astype(BF16)
    a3_hi = a3.astype(BF16).astype(F32)
    lane = jax.lax.broadcasted_iota(jnp.int32, a3.shape, 1)
    mid = (lane >= GLA_GATE_RANK) & (lane < 2 * GLA_GATE_RANK)
    z = _dot(jnp.where(mid, a3 - a3_hi, a3_hi).astype(BF16), wg_ref[...]) + bg_ref[...]
    pu = proj(OFF_U, D_MODEL)
    log_f = (jnp.minimum(z, 0.0) - jnp.log(1.0 + jnp.exp(-jnp.abs(z)))) * (LOG2E / GLA_GATE_TEMP)

    row = jax.lax.broadcasted_iota(jnp.int32, (SG_CHUNK, SG_CHUNK), 0)
    col = jax.lax.broadcasted_iota(jnp.int32, (SG_CHUNK, SG_CHUNK), 1)
    mixed_g = []
    for g in range(SG_GROUPS):
        w_g = jnp.where(row >= col, ws_ref[g], 0.0).astype(BF16)
        gs = slice(g * LANES, (g + 1) * LANES)
        rhs = jnp.concatenate([v_sg[c * SG_CHUNK:(c + 1) * SG_CHUNK, gs] for c in range(MIX_BLOCK // SG_CHUNK)],
                              axis=1)
        mixed_g.append(_dot(w_g, rhs))
    mixed = jnp.concatenate(
        [jnp.concatenate([mg[:, c * LANES:(c + 1) * LANES] for mg in mixed_g], axis=1) + bs_ref[...]
         for c in range(MIX_BLOCK // SG_CHUNK)], axis=0)
    pga = proj(OFF_GA, D_MODEL)

    decays, prefixes, suffixes = [], [], []
    for c in range(n_chunks):
        f_hi, f_lo = _split_bf16(log_f[c * GLA_CHUNK:(c + 1) * GLA_CHUNK])
        expo = _dot(tab_ref[...], jnp.concatenate([f_hi, f_lo], axis=0))
        cum = expo[GLA_LEVELS * GLA_CHUNK:]
        decays.append(jnp.exp2(expo[:GLA_LEVELS * GLA_CHUNK]))
        prefixes.append(jnp.exp2(cum))
        suffixes.append(jnp.exp2(cum[GLA_CHUNK - 1:GLA_CHUNK] - cum))
    m_acc = _sigmoid(pga) * (_gelu(pu) * mixed)
    gv = proj(OFF_GV, GLA_VAL_DIM).astype(BF16)

    lvl2 = lvl_ref[...]
    klane = jax.lax.broadcasted_iota(jnp.int32, (GLA_CHUNK, GLA_KEY_DIM), 1)
    even_head = (klane // GLA_HEAD_K) % 2 == 0
    keep_even = jnp.where(even_head, 1.0, 0.0).astype(BF16)
    keep_odd = jnp.where(even_head, 0.0, 1.0).astype(BF16)
    pair_w = 2 * GLA_HEAD_K

    def pair_scores(lhs, rhs):
        bd = jnp.concatenate([rhs * keep_even, rhs * keep_odd], axis=0)
        return [_dot_nt(lhs[:, p * pair_w:(p + 1) * pair_w], bd[:, p * pair_w:(p + 1) * pair_w])
                for p in range(GLA_HEADS // 2)]

    def chunk_scores(c):
        qc = q[c * GLA_CHUNK:(c + 1) * GLA_CHUNK]
        kc = k[c * GLA_CHUNK:(c + 1) * GLA_CHUNK]
        scores = [jnp.zeros((GLA_CHUNK, 2 * GLA_CHUNK), F32) for _ in range(GLA_HEADS // 2)]
        for lev in range(GLA_LEVELS):
            w = GLA_CHUNK >> (lev + 1)
            n_blk = GLA_CHUNK // w
            if w >= 8:
                mixed_qk = jnp.concatenate([(qc if (b % 2) else kc)[b * w:(b + 1) * w] for b in range(n_blk)],
                                           axis=0)
            else:
                mixed_qk = jnp.where(upper_ref[GLA_LEVELS - 1 - lev] != 0, qc, kc)
            zl = (mixed_qk * decays[c][lev * GLA_CHUNK:(lev + 1) * GLA_CHUNK]).astype(BF16)
            if w >= BF16_ROWS:
                lhs = jnp.concatenate([zl[b * w:(b + 1) * w] for b in range(1, n_blk, 2)], axis=0)
                parts = pair_scores(lhs, zl)
                for p in range(GLA_HEADS // 2):
                    rows = []
                    for b in range(n_blk):
                        old = scores[p][b * w:(b + 1) * w]
                        if b % 2:
                            new = parts[p][(b // 2) * w:(b // 2 + 1) * w]
                            old = jnp.where(lvl2[b * w:(b + 1) * w] == lev, new, old)
                        rows.append(old)
                    scores[p] = jnp.concatenate(rows, axis=0)
            else:
                parts = pair_scores(zl, zl)
                for p in range(GLA_HEADS // 2):
                    scores[p] = jnp.where(lvl2 == lev, parts[p], scores[p])
        qk = qc * kc
        for p in range(GLA_HEADS // 2):
            diag = [jnp.broadcast_to(jnp.sum(qk[:, h * GLA_HEAD_K:(h + 1) * GLA_HEAD_K], axis=1, keepdims=True),
                                     (GLA_CHUNK, GLA_CHUNK)) for h in (2 * p, 2 * p + 1)]
            scores[p] = jnp.where(lvl2 == GLA_LEVELS, jnp.concatenate(diag, axis=1), scores[p])
        return scores

    all_scores = [chunk_scores(c) for c in range(n_chunks)]
    r = proj(OFF_R, GLA_VAL_DIM)

    o_rows = []
    for c in range(n_chunks):
        r0 = c * GLA_CHUNK
        q_in = (q[r0:r0 + GLA_CHUNK] * prefixes[c]).astype(BF16)
        k_out = (k[r0:r0 + GLA_CHUNK] * suffixes[c]).astype(BF16)
        total = prefixes[c][GLA_CHUNK - 1:GLA_CHUNK]
        heads = []
        for h in range(GLA_HEADS):
            sl = slice(h * GLA_HEAD_K, (h + 1) * GLA_HEAD_K)
            vh = gv[r0:r0 + GLA_CHUNK, h * GLA_HEAD_V:(h + 1) * GLA_HEAD_V]
            st = state_ref[h]
            s_h = all_scores[c][h // 2][:, (h % 2) * GLA_CHUNK:(h % 2 + 1) * GLA_CHUNK].astype(BF16)
            o_h = _dot(jnp.concatenate([s_h, q_in[:, sl]], axis=1),
                       jnp.concatenate([vh, st.astype(BF16)], axis=0))
            tcol = jnp.transpose(jnp.broadcast_to(total[:, sl], (GLA_HEAD_K, GLA_HEAD_K)))
            state_ref[h] = st * jnp.concatenate([tcol, tcol], axis=1) + _dot_tn(k_out[:, sl], vh)
            o_h = o_h * jax.lax.rsqrt(jnp.mean(o_h * o_h, axis=-1, keepdims=True) + LN_EPS)
            heads.append(o_h)
        o_rows.append(jnp.concatenate(heads, axis=1))
    pgb = proj(OFF_GB, D_MODEL)
    y_b = (jnp.concatenate(o_rows, axis=0) * gn_ref[...]) * (r * _sigmoid(r))
    m = m_acc + _sigmoid(pgb) * y_b

    h1 = alpha * x + _dot(m.astype(BF16), wout_ref[...])
    o_ref[0] = _layer_norm(h1, ln_g_ref[...], ln_b_ref[...])


def _ffn_kernel(h_ref, w1_ref, w2_ref, ln_g_ref, ln_b_ref, o_ref, *, alpha):
    h = h_ref[...]
    hb = h.astype(BF16)
    acc = alpha * h
    for j in range(D_FF // D_MODEL):
        a = jnp.maximum(_dot(hb, w1_ref[:, j * D_MODEL:(j + 1) * D_MODEL]), 0.0)
        acc = acc + _dot((a * a).astype(BF16), w2_ref[j * D_MODEL:(j + 1) * D_MODEL, :])
    o_ref[...] = _layer_norm(acc, ln_g_ref[...], ln_b_ref[...])


def _resident(shape):
    nd = len(shape)
    return pl.BlockSpec(shape, lambda *_: (0,) * nd, pipeline_mode=pl.Buffered(1))


def _mixer_call(x, consts, alpha):
    bsz, t, d = x.shape
    return pl.pallas_call(
        functools.partial(_mixer_kernel, alpha=alpha),
        out_shape=jax.ShapeDtypeStruct((bsz, t, d), F32),
        grid=(bsz, t // MIX_BLOCK),
        in_specs=[pl.BlockSpec((1, MIX_BLOCK, d), lambda b, i: (b, i, 0))]
        + [_resident(c.shape) for c in consts],
        out_specs=pl.BlockSpec((1, MIX_BLOCK, d), lambda b, i: (b, i, 0)),
        scratch_shapes=[pltpu.VMEM((GLA_HEADS, GLA_HEAD_K, GLA_HEAD_V), F32)],
        compiler_params=pltpu.CompilerParams(
            dimension_semantics=("arbitrary", "arbitrary"), vmem_limit_bytes=VMEM_LIMIT),
        name="mixer",
    )(x, *consts)


def _ffn_call(h, w1, w2, ln_g, ln_b, alpha):
    n, d = h.shape
    return pl.pallas_call(
        functools.partial(_ffn_kernel, alpha=alpha),
        out_shape=jax.ShapeDtypeStruct((n, d), F32),
        grid=(n // FFN_BLOCK,),
        in_specs=[pl.BlockSpec((FFN_BLOCK, d), lambda i: (i, 0)),
                  _resident(w1.shape), _resident(w2.shape), _resident(ln_g.shape), _resident(ln_b.shape)],
        out_specs=pl.BlockSpec((FFN_BLOCK, d), lambda i: (i, 0)),
        compiler_params=pltpu.CompilerParams(
            dimension_semantics=("arbitrary",), vmem_limit_bytes=VMEM_LIMIT),
        name="ffn",
    )(h, w1, w2, ln_g, ln_b)


def kernel(x, w_in, b_in, sg_ln_g, sg_ln_b, sg_w_s, sg_b_s, gla_w_gate2, gla_b_gate, gla_norm_g,
           w_out, ln1_g, ln1_b, w_ff1, w_ff2, ln2_g, ln2_b):
    bsz, t, d = x.shape
    depth = w_in.shape[0]
    alpha = (2.0 * depth) ** 0.25
    table_np, level_np = _decay_tables()
    table = jnp.asarray(np.concatenate([table_np, table_np], axis=1), BF16)
    level = jnp.asarray(np.concatenate([level_np, level_np], axis=1))
    upper = jnp.asarray(np.stack([np.broadcast_to(((np.arange(GLA_CHUNK) >> j) & 1)[:, None],
                                                  (GLA_CHUNK, GLA_KEY_DIM)) for j in range(3)]).astype(np.int32))
    low0, low1 = OFF_LOW, OFF_LOW + GLA_GATE_RANK
    rank = GLA_GATE_RANK
    h = x
    for l in range(depth):
        w_main = jnp.transpose(w_in[l]).astype(BF16)
        b_main = jnp.concatenate([b_in[l][:low0], b_in[l][low1:]])[None, :]
        w_low = jnp.pad(jnp.tile(w_main[low0:low1], (3, 1)), ((0, LANES - 3 * rank), (0, 0)))
        b_low = jnp.pad(jnp.tile(b_in[l][low0:low1], 3), (0, LANES - 3 * rank))[None, :]
        wg_hi = gla_w_gate2[l].astype(BF16)
        wg_lo = (gla_w_gate2[l] - wg_hi.astype(F32)).astype(BF16)
        wg = jnp.pad(jnp.concatenate([wg_hi, wg_hi, wg_lo], axis=0), ((0, LANES - 3 * rank), (0, 0)))
        bs_plane = jnp.repeat(jnp.transpose(sg_b_s[l]), D_MODEL // SG_GROUPS, axis=1)
        consts = (w_main, b_main, w_low, b_low, wg, gla_b_gate[l][None, :],
                  sg_ln_g[l][None, :], sg_ln_b[l][None, :], sg_w_s[l], bs_plane, table, level, upper,
                  gla_norm_g[l][None, :], w_out[l].astype(BF16), ln1_g[l][None, :], ln1_b[l][None, :])
        h = _mixer_call(h, consts, alpha)
        h = _ffn_call(h.reshape(bsz * t, d), w_ff1[l].astype(BF16), w_ff2[l].astype(BF16),
                      ln2_g[l][None, :], ln2_b[l][None, :], alpha).reshape(bsz, t, d)
    return h
```

```python
import functools
import math

import jax
import jax.numpy as jnp
import numpy as np
from jax.experimental import pallas as pl
from jax.experimental.pallas import tpu as pltpu

D_MODEL = 1024
SG_CHUNK = 128
SG_GROUPS = 8
GLA_HEADS = 4
GLA_KEY_DIM = 512
GLA_VAL_DIM = 1024
GLA_HEAD_K = GLA_KEY_DIM // GLA_HEADS
GLA_HEAD_V = GLA_VAL_DIM // GLA_HEADS
GLA_GATE_RANK = 16
GLA_GATE_TEMP = 16.0
D_FF = 4 * D_MODEL
LN_EPS = 1e-5

LANES = 128
BF16_ROWS = 16
GLA_CHUNK = 128
GLA_LEVELS = int(math.log2(GLA_CHUNK))
MIX_BLOCK = 512
FFN_BLOCK = 1024
VMEM_LIMIT = 56 * 1024 * 1024

OFF_U, OFF_V, OFF_Q, OFF_K, OFF_GV, OFF_R, OFF_LOW, OFF_GA, OFF_GB = 0, 1024, 2048, 2560, 3072, 4096, 5120, 5136, 6160

GELU_C = 0.7978845608028654
LOG2E = 1.4426950408889634

BF16 = jnp.bfloat16
F32 = jnp.float32


def _decay_tables():
    c = GLA_CHUNK
    blocks = []
    for lw in range(GLA_LEVELS - 1, -1, -1):
        a = np.zeros((c, c), np.float32)
        for t in range(c):
            if (t >> lw) & 1:
                a[t, (t >> lw) << lw:t + 1] = 1.0
            else:
                a[t, t + 1:((t >> lw) + 1) << lw] = 1.0
        blocks.append(a)
    blocks.append(np.tril(np.ones((c, c), np.float32)))
    table = np.concatenate(blocks, axis=0)
    t = np.arange(c)[:, None]
    s = np.arange(c)[None, :]
    x = t ^ s
    top = np.zeros_like(x)
    nz = x > 0
    top[nz] = np.floor(np.log2(x[nz])).astype(x.dtype)
    level = np.where(t > s, GLA_LEVELS - 1 - top, np.where(t == s, GLA_LEVELS, -1)).astype(np.int32)
    return table, level


def _layer_norm(x, g, b):
    mu = jnp.mean(x, axis=-1, keepdims=True)
    xc = x - mu
    var = jnp.mean(xc * xc, axis=-1, keepdims=True)
    return xc * jax.lax.rsqrt(var + LN_EPS) * g + b


def _gelu(x):
    hx = 0.5 * x
    return hx + hx * jnp.tanh(x * (GELU_C + (GELU_C * 0.044715) * (x * x)))


def _sigmoid(x):
    return 0.5 * jnp.tanh(0.5 * x) + 0.5


def _split_bf16(x):
    hi = x.astype(BF16)
    lo = (x - hi.astype(F32)).astype(BF16)
    return hi, lo


def _dot(a, b):
    return jnp.dot(a, b, preferred_element_type=F32)


def _dot_nt(a, b):
    return jax.lax.dot_general(a, b, (((1,), (1,)), ((), ())), preferred_element_type=F32)


def _dot_tn(a, b):
    return jax.lax.dot_general(a, b, (((0,), (0,)), ((), ())), preferred_element_type=F32)


def _mixer_kernel(x_ref, w1f_ref, w2f_ref, w_ref, b_ref, wlow_ref, blow_ref, wg_ref, bg_ref,
                  sg_g_ref, sg_b_ref, ws_ref, bs_ref, tab_ref, lvl_ref, upper_ref, gn_ref,
                  wout_ref, ln_g_ref, ln_b_ref, o_ref, w1b_ref, w2b_ref, state_ref, *, alpha):
    @pl.when(pl.program_id(1) == 0)
    def _():
        state_ref[...] = jnp.zeros_like(state_ref)

    w1b_ref[...] = w1f_ref[...].astype(BF16)
    w2b_ref[...] = w2f_ref[...].astype(BF16)

    x = x_ref[0]
    xh = x.astype(BF16)
    n_chunks = MIX_BLOCK // GLA_CHUNK

    def proj(off, width):
        b_off = off if off < OFF_LOW else off - GLA_GATE_RANK
        return _dot_nt(xh, w_ref[off:off + width, :]) + b_ref[:, b_off:b_off + width]

    pv = proj(OFF_V, D_MODEL)
    q = proj(OFF_Q, GLA_KEY_DIM) * (GLA_HEAD_K ** -0.5)
    k = proj(OFF_K, GLA_KEY_DIM)
    a3 = (_dot_nt(xh, wlow_ref[...]) + blow_ref[...])[:, :LANES]

    v_sg = _layer_norm(_gelu(pv), sg_g_ref[...], sg_b_ref[...]).astype(BF16)
    a3_hi = a3.astype(BF16).astype(F32)
    lane = jax.lax.broadcasted_iota(jnp.int32, a3.shape, 1)
    mid = (lane >= GLA_GATE_RANK) & (lane < 2 * GLA_GATE_RANK)
    z = _dot(jnp.where(mid, a3 - a3_hi, a3_hi).astype(BF16), wg_ref[...]) + bg_ref[...]
    pu = proj(OFF_U, D_MODEL)
    log_f = (jnp.minimum(z, 0.0) - jnp.log(1.0 + jnp.exp(-jnp.abs(z)))) * (LOG2E / GLA_GATE_TEMP)

    row = jax.lax.broadcasted_iota(jnp.int32, (SG_CHUNK, SG_CHUNK), 0)
    col = jax.lax.broadcasted_iota(jnp.int32, (SG_CHUNK, SG_CHUNK), 1)
    mixed_g = []
    for g in range(SG_GROUPS):
        w_g = jnp.where(row >= col, ws_ref[g], 0.0).astype(BF16)
        gs = slice(g * LANES, (g + 1) * LANES)
        rhs = jnp.concatenate([v_sg[c * SG_CHUNK:(c + 1) * SG_CHUNK, gs] for c in range(MIX_BLOCK // SG_CHUNK)],
                              axis=1)
        mixed_g.append(_dot(w_g, rhs))
    mixed = jnp.concatenate(
        [jnp.concatenate([mg[:, c * LANES:(c + 1) * LANES] for mg in mixed_g], axis=1) + bs_ref[...]
         for c in range(MIX_BLOCK // SG_CHUNK)], axis=0)
    pga = proj(OFF_GA, D_MODEL)

    decays, prefixes, suffixes = [], [], []
    for c in range(n_chunks):
        f_hi, f_lo = _split_bf16(log_f[c * GLA_CHUNK:(c + 1) * GLA_CHUNK])
        expo = _dot(tab_ref[...], jnp.concatenate([f_hi, f_lo], axis=0))
        cum = expo[GLA_LEVELS * GLA_CHUNK:]
        decays.append(jnp.exp2(expo[:GLA_LEVELS * GLA_CHUNK]))
        prefixes.append(jnp.exp2(cum))
        suffixes.append(jnp.exp2(cum[GLA_CHUNK - 1:GLA_CHUNK] - cum))
    m_acc = _sigmoid(pga) * (_gelu(pu) * mixed)

    lvl2 = lvl_ref[...]
    klane = jax.lax.broadcasted_iota(jnp.int32, (GLA_CHUNK, GLA_KEY_DIM), 1)
    even_head = (klane // GLA_HEAD_K) % 2 == 0
    keep_even = jnp.where(even_head, 1.0, 0.0).astype(BF16)
    keep_odd = jnp.where(even_head, 0.0, 1.0).astype(BF16)
    pair_w = 2 * GLA_HEAD_K

    def pair_scores(lhs, rhs):
        bd = jnp.concatenate([rhs * keep_even, rhs * keep_odd], axis=0)
        return [_dot_nt(lhs[:, p * pair_w:(p + 1) * pair_w], bd[:, p * pair_w:(p + 1) * pair_w])
                for p in range(GLA_HEADS // 2)]

    def level_update(scores, c, lev):
        qc = q[c * GLA_CHUNK:(c + 1) * GLA_CHUNK]
        kc = k[c * GLA_CHUNK:(c + 1) * GLA_CHUNK]
        w = GLA_CHUNK >> (lev + 1)
        n_blk = GLA_CHUNK // w
        if w >= 8:
            mixed_qk = jnp.concatenate([(qc if (b % 2) else kc)[b * w:(b + 1) * w] for b in range(n_blk)], axis=0)
        else:
            mixed_qk = jnp.where(upper_ref[GLA_LEVELS - 1 - lev] != 0, qc, kc)
        zl = (mixed_qk * decays[c][lev * GLA_CHUNK:(lev + 1) * GLA_CHUNK]).astype(BF16)
        if w >= BF16_ROWS:
            lhs = jnp.concatenate([zl[b * w:(b + 1) * w] for b in range(1, n_blk, 2)], axis=0)
            parts = pair_scores(lhs, zl)
            out = []
            for p in range(GLA_HEADS // 2):
                rows = []
                for b in range(n_blk):
                    old = scores[p][b * w:(b + 1) * w]
                    if b % 2:
                        new = parts[p][(b // 2) * w:(b // 2 + 1) * w]
                        old = jnp.where(lvl2[b * w:(b + 1) * w] == lev, new, old)
                    rows.append(old)
                out.append(jnp.concatenate(rows, axis=0))
            return out
        parts = pair_scores(zl, zl)
        return [jnp.where(lvl2 == lev, parts[p], scores[p]) for p in range(GLA_HEADS // 2)]

    def diagonal_update(scores, c):
        qk = q[c * GLA_CHUNK:(c + 1) * GLA_CHUNK] * k[c * GLA_CHUNK:(c + 1) * GLA_CHUNK]
        out = []
        for p in range(GLA_HEADS // 2):
            diag = [jnp.broadcast_to(jnp.sum(qk[:, h * GLA_HEAD_K:(h + 1) * GLA_HEAD_K], axis=1, keepdims=True),
                                     (GLA_CHUNK, GLA_CHUNK)) for h in (2 * p, 2 * p + 1)]
            out.append(jnp.where(lvl2 == GLA_LEVELS, jnp.concatenate(diag, axis=1), scores[p]))
        return out

    all_scores = [[jnp.zeros((GLA_CHUNK, 2 * GLA_CHUNK), F32) for _ in range(GLA_HEADS // 2)]
                  for _ in range(n_chunks)]
    for lev in range(GLA_LEVELS):
        all_scores = [level_update(all_scores[c], c, lev) for c in range(n_chunks)]
        if lev == 1:
            gv = proj(OFF_GV, GLA_VAL_DIM).astype(BF16)
        if lev == 4:
            r = proj(OFF_R, GLA_VAL_DIM)
    all_scores = [diagonal_update(all_scores[c], c) for c in range(n_chunks)]

    o_rows = []
    for c in range(n_chunks):
        r0 = c * GLA_CHUNK
        q_in = (q[r0:r0 + GLA_CHUNK] * prefixes[c]).astype(BF16)
        k_out = (k[r0:r0 + GLA_CHUNK] * suffixes[c]).astype(BF16)
        total = prefixes[c][GLA_CHUNK - 1:GLA_CHUNK]
        heads = []
        for h in range(GLA_HEADS):
            sl = slice(h * GLA_HEAD_K, (h + 1) * GLA_HEAD_K)
            vh = gv[r0:r0 + GLA_CHUNK, h * GLA_HEAD_V:(h + 1) * GLA_HEAD_V]
            st = state_ref[h]
            s_h = all_scores[c][h // 2][:, (h % 2) * GLA_CHUNK:(h % 2 + 1) * GLA_CHUNK].astype(BF16)
            o_h = _dot(jnp.concatenate([s_h, q_in[:, sl]], axis=1),
                       jnp.concatenate([vh, st.astype(BF16)], axis=0))
            tcol = jnp.transpose(jnp.broadcast_to(total[:, sl], (GLA_HEAD_K, GLA_HEAD_K)))
            state_ref[h] = st * jnp.concatenate([tcol, tcol], axis=1) + _dot_tn(k_out[:, sl], vh)
            o_h = o_h * jax.lax.rsqrt(jnp.mean(o_h * o_h, axis=-1, keepdims=True) + LN_EPS)
            heads.append(o_h)
        o_rows.append(jnp.concatenate(heads, axis=1))
    pgb = proj(OFF_GB, D_MODEL)
    y_b = (jnp.concatenate(o_rows, axis=0) * gn_ref[...]) * (r * _sigmoid(r))
    m = (m_acc + _sigmoid(pgb) * y_b).astype(BF16)

    half = MIX_BLOCK // 2
    for r0 in (0, half):
        h1 = alpha * x[r0:r0 + half] + _dot(m[r0:r0 + half], wout_ref[...])
        o_ref[0, r0:r0 + half, :] = _layer_norm(h1, ln_g_ref[...], ln_b_ref[...])


def _ffn_kernel(h_ref, w1_ref, w2_ref, ln_g_ref, ln_b_ref, o_ref, *, alpha):
    h = h_ref[...]
    hb = h.astype(BF16)
    acc = alpha * h
    for j in range(D_FF // D_MODEL):
        a = jnp.maximum(_dot(hb, w1_ref[:, j * D_MODEL:(j + 1) * D_MODEL]), 0.0)
        acc = acc + _dot((a * a).astype(BF16), w2_ref[j * D_MODEL:(j + 1) * D_MODEL, :])
    o_ref[...] = _layer_norm(acc, ln_g_ref[...], ln_b_ref[...])


def _resident(shape):
    nd = len(shape)
    return pl.BlockSpec(shape, lambda *_: (0,) * nd, pipeline_mode=pl.Buffered(1))


def _mixer_call(x, w1, w2, consts, alpha):
    bsz, t, d = x.shape
    blocks_per_row = t // MIX_BLOCK
    n_steps = bsz * blocks_per_row
    slab1, slab2 = w1.shape[0] // n_steps, w2.shape[0] // n_steps

    def slab(b, i):
        return (b * blocks_per_row + i, 0)

    return pl.pallas_call(
        functools.partial(_mixer_kernel, alpha=alpha),
        out_shape=(jax.ShapeDtypeStruct((bsz, t, d), F32),
                   jax.ShapeDtypeStruct(w1.shape, BF16), jax.ShapeDtypeStruct(w2.shape, BF16)),
        grid=(bsz, blocks_per_row),
        in_specs=[pl.BlockSpec((1, MIX_BLOCK, d), lambda b, i: (b, i, 0)),
                  pl.BlockSpec((slab1, w1.shape[1]), slab), pl.BlockSpec((slab2, w2.shape[1]), slab)]
        + [_resident(c.shape) for c in consts],
        out_specs=(pl.BlockSpec((1, MIX_BLOCK, d), lambda b, i: (b, i, 0)),
                   pl.BlockSpec((slab1, w1.shape[1]), slab), pl.BlockSpec((slab2, w2.shape[1]), slab)),
        scratch_shapes=[pltpu.VMEM((GLA_HEADS, GLA_HEAD_K, GLA_HEAD_V), F32)],
        compiler_params=pltpu.CompilerParams(
            dimension_semantics=("arbitrary", "arbitrary"), vmem_limit_bytes=VMEM_LIMIT),
        name="mixer",
    )(x, w1, w2, *consts)


def _ffn_call(h, w1, w2, ln_g, ln_b, alpha):
    n, d = h.shape
    return pl.pallas_call(
        functools.partial(_ffn_kernel, alpha=alpha),
        out_shape=jax.ShapeDtypeStruct((n, d), F32),
        grid=(n // FFN_BLOCK,),
        in_specs=[pl.BlockSpec((FFN_BLOCK, d), lambda i: (i, 0)),
                  _resident(w1.shape), _resident(w2.shape), _resident(ln_g.shape), _resident(ln_b.shape)],
        out_specs=pl.BlockSpec((FFN_BLOCK, d), lambda i: (i, 0)),
        compiler_params=pltpu.CompilerParams(
            dimension_semantics=("arbitrary",), vmem_limit_bytes=VMEM_LIMIT),
        name="ffn",
    )(h, w1, w2, ln_g, ln_b)


def kernel(x, w_in, b_in, sg_ln_g, sg_ln_b, sg_w_s, sg_b_s, gla_w_gate2, gla_b_gate, gla_norm_g,
           w_out, ln1_g, ln1_b, w_ff1, w_ff2, ln2_g, ln2_b):
    bsz, t, d = x.shape
    depth = w_in.shape[0]
    alpha = (2.0 * depth) ** 0.25
    table_np, level_np = _decay_tables()
    table = jnp.asarray(np.concatenate([table_np, table_np], axis=1), BF16)
    level = jnp.asarray(np.concatenate([level_np, level_np], axis=1))
    upper = jnp.asarray(np.stack([np.broadcast_to(((np.arange(GLA_CHUNK) >> j) & 1)[:, None],
                                                  (GLA_CHUNK, GLA_KEY_DIM)) for j in range(3)]).astype(np.int32))
    low0, low1 = OFF_LOW, OFF_LOW + GLA_GATE_RANK
    rank = GLA_GATE_RANK
    h = x
    for l in range(depth):
        w_main = jnp.transpose(w_in[l]).astype(BF16)
        b_main = jnp.concatenate([b_in[l][:low0], b_in[l][low1:]])[None, :]
        w_low = jnp.pad(jnp.tile(w_main[low0:low1], (3, 1)), ((0, 2 * LANES - 3 * rank), (0, 0)))
        b_low = jnp.pad(jnp.tile(b_in[l][low0:low1], 3), (0, 2 * LANES - 3 * rank))[None, :]
        wg_hi = gla_w_gate2[l].astype(BF16)
        wg_lo = (gla_w_gate2[l] - wg_hi.astype(F32)).astype(BF16)
        wg = jnp.pad(jnp.concatenate([wg_hi, wg_hi, wg_lo], axis=0), ((0, LANES - 3 * rank), (0, 0)))
        bs_plane = jnp.repeat(jnp.transpose(sg_b_s[l]), D_MODEL // SG_GROUPS, axis=1)
        consts = (w_main, b_main, w_low, b_low, wg, gla_b_gate[l][None, :],
                  sg_ln_g[l][None, :], sg_ln_b[l][None, :], sg_w_s[l], bs_plane, table, level, upper,
                  gla_norm_g[l][None, :], w_out[l].astype(BF16), ln1_g[l][None, :], ln1_b[l][None, :])
        h, w1b, w2b = _mixer_call(h, w_ff1[l], w_ff2[l], consts, alpha)
        h = _ffn_call(h.reshape(bsz * t, d), w1b, w2b,
                      ln2_g[l][None, :], ln2_b[l][None, :], alpha).reshape(bsz, t, d)
    return h
```

```python
import functools
import math

import jax
import jax.numpy as jnp
import numpy as np
from jax.experimental import pallas as pl
from jax.experimental.pallas import tpu as pltpu

D_MODEL = 1024
SG_CHUNK = 128
SG_GROUPS = 8
GLA_HEADS = 4
GLA_KEY_DIM = 512
GLA_VAL_DIM = 1024
GLA_HEAD_K = GLA_KEY_DIM // GLA_HEADS
GLA_HEAD_V = GLA_VAL_DIM // GLA_HEADS
GLA_GATE_RANK = 16
GLA_GATE_TEMP = 16.0
D_FF = 4 * D_MODEL
LN_EPS = 1e-5

LANES = 128
BF16_ROWS = 16
GLA_CHUNK = 128
GLA_LEVELS = int(math.log2(GLA_CHUNK))
MIX_BLOCK = 512
FFN_BLOCK = 1024
W_STAGE_ROWS = 512
VMEM_LIMIT = 56 * 1024 * 1024

OFF_U, OFF_V, OFF_Q, OFF_K, OFF_GV, OFF_R, OFF_LOW, OFF_GA, OFF_GB = 0, 1024, 2048, 2560, 3072, 4096, 5120, 5136, 6160

GELU_C = 0.7978845608028654
LOG2E = 1.4426950408889634

BF16 = jnp.bfloat16
F32 = jnp.float32


def _decay_tables():
    c = GLA_CHUNK
    blocks = []
    for lw in range(GLA_LEVELS - 1, -1, -1):
        a = np.zeros((c, c), np.float32)
        for t in range(c):
            if (t >> lw) & 1:
                a[t, (t >> lw) << lw:t + 1] = 1.0
            else:
                a[t, t + 1:((t >> lw) + 1) << lw] = 1.0
        blocks.append(a)
    blocks.append(np.tril(np.ones((c, c), np.float32)))
    table = np.concatenate(blocks, axis=0)
    t = np.arange(c)[:, None]
    s = np.arange(c)[None, :]
    x = t ^ s
    top = np.zeros_like(x)
    nz = x > 0
    top[nz] = np.floor(np.log2(x[nz])).astype(x.dtype)
    level = np.where(t > s, GLA_LEVELS - 1 - top, np.where(t == s, GLA_LEVELS, -1)).astype(np.int32)
    return table, level


def _layer_norm(x, g, b):
    mu = jnp.mean(x, axis=-1, keepdims=True)
    xc = x - mu
    var = jnp.mean(xc * xc, axis=-1, keepdims=True)
    return xc * jax.lax.rsqrt(var + LN_EPS) * g + b


def _gelu(x):
    hx = 0.5 * x
    return hx + hx * jnp.tanh(x * (GELU_C + (GELU_C * 0.044715) * (x * x)))


def _sigmoid(x):
    return 0.5 * jnp.tanh(0.5 * x) + 0.5


def _split_bf16(x):
    hi = x.astype(BF16)
    lo = (x - hi.astype(F32)).astype(BF16)
    return hi, lo


def _dot(a, b):
    return jnp.dot(a, b, preferred_element_type=F32)


def _dot_nt(a, b):
    return jax.lax.dot_general(a, b, (((1,), (1,)), ((), ())), preferred_element_type=F32)


def _dot_tn(a, b):
    return jax.lax.dot_general(a, b, (((0,), (0,)), ((), ())), preferred_element_type=F32)


def _cast_projection_weights(w_hbm, w_ref, stage_ref, sem):
    n_rows = w_ref.shape[0]
    starts = list(range(0, n_rows, W_STAGE_ROWS))

    def copy(j):
        rows = min(W_STAGE_ROWS, n_rows - starts[j])
        return pltpu.make_async_copy(w_hbm.at[pl.ds(starts[j], rows)],
                                     stage_ref.at[j % 2, pl.ds(0, rows)], sem.at[j % 2])

    copy(0).start()
    for j, r0 in enumerate(starts):
        if j + 1 < len(starts):
            copy(j + 1).start()
        copy(j).wait()
        rows = min(W_STAGE_ROWS, n_rows - r0)
        w_ref[r0:r0 + rows, :] = stage_ref[j % 2, 0:rows, :].astype(BF16)


def _mixer_kernel(x_ref, w1f_ref, w2f_ref, w_hbm, b_ref, wlow_ref, blow_ref, wg_ref, bg_ref,
                  sg_g_ref, sg_b_ref, ws_ref, bs_ref, tab_ref, lvl_ref, upper_ref, gn_ref,
                  wout_ref, ln_g_ref, ln_b_ref, o_ref, w1b_ref, w2b_ref,
                  state_ref, w_ref, stage_ref, sem, *, alpha):
    @pl.when((pl.program_id(0) == 0) & (pl.program_id(1) == 0))
    def _():
        _cast_projection_weights(w_hbm, w_ref, stage_ref, sem)

    @pl.when(pl.program_id(1) == 0)
    def _():
        state_ref[...] = jnp.zeros_like(state_ref)

    w1b_ref[...] = w1f_ref[...].astype(BF16)
    w2b_ref[...] = w2f_ref[...].astype(BF16)

    x = x_ref[0]
    xh = x.astype(BF16)
    n_chunks = MIX_BLOCK // GLA_CHUNK

    def proj(off, width):
        b_off = off if off < OFF_LOW else off - GLA_GATE_RANK
        return _dot_nt(xh, w_ref[off:off + width, :]) + b_ref[:, b_off:b_off + width]

    pv = proj(OFF_V, D_MODEL)
    q = proj(OFF_Q, GLA_KEY_DIM) * (GLA_HEAD_K ** -0.5)
    k = proj(OFF_K, GLA_KEY_DIM)
    a3 = (_dot_nt(xh, wlow_ref[...]) + blow_ref[...])[:, :LANES]

    v_sg = _layer_norm(_gelu(pv), sg_g_ref[...], sg_b_ref[...]).astype(BF16)
    a3_hi = a3.astype(BF16).astype(F32)
    lane = jax.lax.broadcasted_iota(jnp.int32, a3.shape, 1)
    mid = (lane >= GLA_GATE_RANK) & (lane < 2 * GLA_GATE_RANK)
    z = _dot(jnp.where(mid, a3 - a3_hi, a3_hi).astype(BF16), wg_ref[...]) + bg_ref[...]
    pu = proj(OFF_U, D_MODEL)
    log_f = (jnp.minimum(z, 0.0) - jnp.log(1.0 + jnp.exp(-jnp.abs(z)))) * (LOG2E / GLA_GATE_TEMP)

    row = jax.lax.broadcasted_iota(jnp.int32, (SG_CHUNK, SG_CHUNK), 0)
    col = jax.lax.broadcasted_iota(jnp.int32, (SG_CHUNK, SG_CHUNK), 1)
    mixed_g = []
    for g in range(SG_GROUPS):
        w_g = jnp.where(row >= col, ws_ref[g], 0.0).astype(BF16)
        gs = slice(g * LANES, (g + 1) * LANES)
        rhs = jnp.concatenate([v_sg[c * SG_CHUNK:(c + 1) * SG_CHUNK, gs] for c in range(MIX_BLOCK // SG_CHUNK)],
                              axis=1)
        mixed_g.append(_dot(w_g, rhs))
    mixed = jnp.concatenate(
        [jnp.concatenate([mg[:, c * LANES:(c + 1) * LANES] for mg in mixed_g], axis=1) + bs_ref[...]
         for c in range(MIX_BLOCK // SG_CHUNK)], axis=0)
    pga = proj(OFF_GA, D_MODEL)

    decays, prefixes, suffixes = [], [], []
    for c in range(n_chunks):
        f_hi, f_lo = _split_bf16(log_f[c * GLA_CHUNK:(c + 1) * GLA_CHUNK])
        expo = _dot(tab_ref[...], jnp.concatenate([f_hi, f_lo], axis=0))
        cum = expo[GLA_LEVELS * GLA_CHUNK:]
        decays.append(jnp.exp2(expo[:GLA_LEVELS * GLA_CHUNK]))
        prefixes.append(jnp.exp2(cum))
        suffixes.append(jnp.exp2(cum[GLA_CHUNK - 1:GLA_CHUNK] - cum))
    m_acc = _sigmoid(pga) * (_gelu(pu) * mixed)

    lvl2 = lvl_ref[...]
    klane = jax.lax.broadcasted_iota(jnp.int32, (GLA_CHUNK, GLA_KEY_DIM), 1)
    even_head = (klane // GLA_HEAD_K) % 2 == 0
    keep_even = jnp.where(even_head, 1.0, 0.0).astype(BF16)
    keep_odd = jnp.where(even_head, 0.0, 1.0).astype(BF16)
    pair_w = 2 * GLA_HEAD_K

    def pair_scores(lhs, rhs):
        bd = jnp.concatenate([rhs * keep_even, rhs * keep_odd], axis=0)
        return [_dot_nt(lhs[:, p * pair_w:(p + 1) * pair_w], bd[:, p * pair_w:(p + 1) * pair_w])
                for p in range(GLA_HEADS // 2)]

    def level_update(scores, c, lev):
        qc = q[c * GLA_CHUNK:(c + 1) * GLA_CHUNK]
        kc = k[c * GLA_CHUNK:(c + 1) * GLA_CHUNK]
        w = GLA_CHUNK >> (lev + 1)
        n_blk = GLA_CHUNK // w
        if w >= 8:
            mixed_qk = jnp.concatenate([(qc if (b % 2) else kc)[b * w:(b + 1) * w] for b in range(n_blk)], axis=0)
        else:
            mixed_qk = jnp.where(upper_ref[GLA_LEVELS - 1 - lev] != 0, qc, kc)
        zl = (mixed_qk * decays[c][lev * GLA_CHUNK:(lev + 1) * GLA_CHUNK]).astype(BF16)
        if w >= BF16_ROWS:
            lhs = jnp.concatenate([zl[b * w:(b + 1) * w] for b in range(1, n_blk, 2)], axis=0)
            parts = pair_scores(lhs, zl)
            out = []
            for p in range(GLA_HEADS // 2):
                rows = []
                for b in range(n_blk):
                    old = scores[p][b * w:(b + 1) * w]
                    if b % 2:
                        new = parts[p][(b // 2) * w:(b // 2 + 1) * w]
                        old = jnp.where(lvl2[b * w:(b + 1) * w] == lev, new, old)
                    rows.append(old)
                out.append(jnp.concatenate(rows, axis=0))
            return out
        parts = pair_scores(zl, zl)
        return [jnp.where(lvl2 == lev, parts[p], scores[p]) for p in range(GLA_HEADS // 2)]

    def diagonal_update(scores, c):
        qk = q[c * GLA_CHUNK:(c + 1) * GLA_CHUNK] * k[c * GLA_CHUNK:(c + 1) * GLA_CHUNK]
        out = []
        for p in range(GLA_HEADS // 2):
            diag = [jnp.broadcast_to(jnp.sum(qk[:, h * GLA_HEAD_K:(h + 1) * GLA_HEAD_K], axis=1, keepdims=True),
                                     (GLA_CHUNK, GLA_CHUNK)) for h in (2 * p, 2 * p + 1)]
            out.append(jnp.where(lvl2 == GLA_LEVELS, jnp.concatenate(diag, axis=1), scores[p]))
        return out

    all_scores = [[jnp.zeros((GLA_CHUNK, 2 * GLA_CHUNK), F32) for _ in range(GLA_HEADS // 2)]
                  for _ in range(n_chunks)]
    for lev in range(GLA_LEVELS):
        all_scores = [level_update(all_scores[c], c, lev) for c in range(n_chunks)]
        if lev == 1:
            gv = proj(OFF_GV, GLA_VAL_DIM).astype(BF16)
        if lev == 4:
            r = proj(OFF_R, GLA_VAL_DIM)
    all_scores = [diagonal_update(all_scores[c], c) for c in range(n_chunks)]

    o_rows = []
    for c in range(n_chunks):
        r0 = c * GLA_CHUNK
        q_in = (q[r0:r0 + GLA_CHUNK] * prefixes[c]).astype(BF16)
        k_out = (k[r0:r0 + GLA_CHUNK] * suffixes[c]).astype(BF16)
        total = prefixes[c][GLA_CHUNK - 1:GLA_CHUNK]
        heads = []
        for h in range(GLA_HEADS):
            sl = slice(h * GLA_HEAD_K, (h + 1) * GLA_HEAD_K)
            vh = gv[r0:r0 + GLA_CHUNK, h * GLA_HEAD_V:(h + 1) * GLA_HEAD_V]
            st = state_ref[h]
            s_h = all_scores[c][h // 2][:, (h % 2) * GLA_CHUNK:(h % 2 + 1) * GLA_CHUNK].astype(BF16)
            o_h = _dot(jnp.concatenate([s_h, q_in[:, sl]], axis=1),
                       jnp.concatenate([vh, st.astype(BF16)], axis=0))
            tcol = jnp.transpose(jnp.broadcast_to(total[:, sl], (GLA_HEAD_K, GLA_HEAD_K)))
            state_ref[h] = st * jnp.concatenate([tcol, tcol], axis=1) + _dot_tn(k_out[:, sl], vh)
            o_h = o_h * jax.lax.rsqrt(jnp.mean(o_h * o_h, axis=-1, keepdims=True) + LN_EPS)
            heads.append(o_h)
        o_rows.append(jnp.concatenate(heads, axis=1))
    pgb = proj(OFF_GB, D_MODEL)
    y_b = (jnp.concatenate(o_rows, axis=0) * gn_ref[...]) * (r * _sigmoid(r))
    m = (m_acc + _sigmoid(pgb) * y_b).astype(BF16)

    half = MIX_BLOCK // 2
    for r0 in (0, half):
        h1 = alpha * x[r0:r0 + half] + _dot(m[r0:r0 + half], wout_ref[...])
        o_ref[0, r0:r0 + half, :] = _layer_norm(h1, ln_g_ref[...], ln_b_ref[...])


def _ffn_kernel(h_ref, w1_ref, w2_ref, ln_g_ref, ln_b_ref, o_ref, *, alpha):
    h = h_ref[...]
    hb = h.astype(BF16)
    acc = alpha * h
    for j in range(D_FF // D_MODEL):
        a = jnp.maximum(_dot(hb, w1_ref[:, j * D_MODEL:(j + 1) * D_MODEL]), 0.0)
        acc = acc + _dot((a * a).astype(BF16), w2_ref[j * D_MODEL:(j + 1) * D_MODEL, :])
    o_ref[...] = _layer_norm(acc, ln_g_ref[...], ln_b_ref[...])


def _resident(shape):
    nd = len(shape)
    return pl.BlockSpec(shape, lambda *_: (0,) * nd, pipeline_mode=pl.Buffered(1))


def _mixer_call(x, w1, w2, consts, alpha):
    bsz, t, d = x.shape
    blocks_per_row = t // MIX_BLOCK
    n_steps = bsz * blocks_per_row
    slab1, slab2 = w1.shape[0] // n_steps, w2.shape[0] // n_steps

    def slab(b, i):
        return (b * blocks_per_row + i, 0)

    return pl.pallas_call(
        functools.partial(_mixer_kernel, alpha=alpha),
        out_shape=(jax.ShapeDtypeStruct((bsz, t, d), F32),
                   jax.ShapeDtypeStruct(w1.shape, BF16), jax.ShapeDtypeStruct(w2.shape, BF16)),
        grid=(bsz, blocks_per_row),
        in_specs=[pl.BlockSpec((1, MIX_BLOCK, d), lambda b, i: (b, i, 0)),
                  pl.BlockSpec((slab1, w1.shape[1]), slab), pl.BlockSpec((slab2, w2.shape[1]), slab)]
        + [pl.BlockSpec(memory_space=pl.ANY)] + [_resident(c.shape) for c in consts[1:]],
        out_specs=(pl.BlockSpec((1, MIX_BLOCK, d), lambda b, i: (b, i, 0)),
                   pl.BlockSpec((slab1, w1.shape[1]), slab), pl.BlockSpec((slab2, w2.shape[1]), slab)),
        scratch_shapes=[pltpu.VMEM((GLA_HEADS, GLA_HEAD_K, GLA_HEAD_V), F32),
                        pltpu.VMEM(consts[0].shape, BF16),
                        pltpu.VMEM((2, W_STAGE_ROWS, d), F32),
                        pltpu.SemaphoreType.DMA((2,))],
        compiler_params=pltpu.CompilerParams(
            dimension_semantics=("arbitrary", "arbitrary"), vmem_limit_bytes=VMEM_LIMIT),
        name="mixer",
    )(x, w1, w2, *consts)


def _ffn_call(h, w1, w2, ln_g, ln_b, alpha):
    n, d = h.shape
    return pl.pallas_call(
        functools.partial(_ffn_kernel, alpha=alpha),
        out_shape=jax.ShapeDtypeStruct((n, d), F32),
        grid=(n // FFN_BLOCK,),
        in_specs=[pl.BlockSpec((FFN_BLOCK, d), lambda i: (i, 0)),
                  _resident(w1.shape), _resident(w2.shape), _resident(ln_g.shape), _resident(ln_b.shape)],
        out_specs=pl.BlockSpec((FFN_BLOCK, d), lambda i: (i, 0)),
        compiler_params=pltpu.CompilerParams(
            dimension_semantics=("arbitrary",), vmem_limit_bytes=VMEM_LIMIT),
        name="ffn",
    )(h, w1, w2, ln_g, ln_b)


def kernel(x, w_in, b_in, sg_ln_g, sg_ln_b, sg_w_s, sg_b_s, gla_w_gate2, gla_b_gate, gla_norm_g,
           w_out, ln1_g, ln1_b, w_ff1, w_ff2, ln2_g, ln2_b):
    bsz, t, d = x.shape
    depth = w_in.shape[0]
    alpha = (2.0 * depth) ** 0.25
    table_np, level_np = _decay_tables()
    table = jnp.asarray(np.concatenate([table_np, table_np], axis=1), BF16)
    level = jnp.asarray(np.concatenate([level_np, level_np], axis=1))
    upper = jnp.asarray(np.stack([np.broadcast_to(((np.arange(GLA_CHUNK) >> j) & 1)[:, None],
                                                  (GLA_CHUNK, GLA_KEY_DIM)) for j in range(3)]).astype(np.int32))
    low0, low1 = OFF_LOW, OFF_LOW + GLA_GATE_RANK
    rank = GLA_GATE_RANK
    h = x
    for l in range(depth):
        w_main = jnp.transpose(w_in[l])
        b_main = jnp.concatenate([b_in[l][:low0], b_in[l][low1:]])[None, :]
        w_low = jnp.pad(jnp.tile(w_main[low0:low1].astype(BF16), (3, 1)), ((0, 2 * LANES - 3 * rank), (0, 0)))
        b_low = jnp.pad(jnp.tile(b_in[l][low0:low1], 3), (0, 2 * LANES - 3 * rank))[None, :]
        wg_hi = gla_w_gate2[l].astype(BF16)
        wg_lo = (gla_w_gate2[l] - wg_hi.astype(F32)).astype(BF16)
        wg = jnp.pad(jnp.concatenate([wg_hi, wg_hi, wg_lo], axis=0), ((0, LANES - 3 * rank), (0, 0)))
        bs_plane = jnp.repeat(jnp.transpose(sg_b_s[l]), D_MODEL // SG_GROUPS, axis=1)
        consts = (w_main, b_main, w_low, b_low, wg, gla_b_gate[l][None, :],
                  sg_ln_g[l][None, :], sg_ln_b[l][None, :], sg_w_s[l], bs_plane, table, level, upper,
                  gla_norm_g[l][None, :], w_out[l].astype(BF16), ln1_g[l][None, :], ln1_b[l][None, :])
        h, w1b, w2b = _mixer_call(h, w_ff1[l], w_ff2[l], consts, alpha)
        h = _ffn_call(h.reshape(bsz * t, d), w1b, w2b,
                      ln2_g[l][None, :], ln2_b[l][None, :], alpha).reshape(bsz, t, d)
    return h
```

```python
import functools
import math

import jax
import jax.numpy as jnp
import numpy as np
from jax.experimental import pallas as pl
from jax.experimental.pallas import tpu as pltpu

D_MODEL = 1024
SG_CHUNK = 128
SG_GROUPS = 8
GLA_HEADS = 4
GLA_KEY_DIM = 512
GLA_VAL_DIM = 1024
GLA_HEAD_K = GLA_KEY_DIM // GLA_HEADS
GLA_HEAD_V = GLA_VAL_DIM // GLA_HEADS
GLA_GATE_RANK = 16
GLA_GATE_TEMP = 16.0
D_FF = 4 * D_MODEL
LN_EPS = 1e-5

LANES = 128
BF16_ROWS = 16
GLA_CHUNK = 128
GLA_LEVELS = int(math.log2(GLA_CHUNK))
MIX_BLOCK = 512
FFN_BLOCK = 1024
MIX_TAIL_PIECES = 2
FFN_TAIL_PIECES = 4
W_STAGE_ROWS = 512
VMEM_LIMIT = 56 * 1024 * 1024

OFF_U, OFF_V, OFF_Q, OFF_K, OFF_GV, OFF_R, OFF_LOW, OFF_GA, OFF_GB = 0, 1024, 2048, 2560, 3072, 4096, 5120, 5136, 6160
D_MAIN = 7168

GELU_C = 0.7978845608028654
LOG2E = 1.4426950408889634

BF16 = jnp.bfloat16
F32 = jnp.float32


def _decay_tables():
    c = GLA_CHUNK
    blocks = []
    for lw in range(GLA_LEVELS - 1, -1, -1):
        a = np.zeros((c, c), np.float32)
        for t in range(c):
            if (t >> lw) & 1:
                a[t, (t >> lw) << lw:t + 1] = 1.0
            else:
                a[t, t + 1:((t >> lw) + 1) << lw] = 1.0
        blocks.append(a)
    blocks.append(np.tril(np.ones((c, c), np.float32)))
    table = np.concatenate(blocks, axis=0)
    t = np.arange(c)[:, None]
    s = np.arange(c)[None, :]
    x = t ^ s
    top = np.zeros_like(x)
    nz = x > 0
    top[nz] = np.floor(np.log2(x[nz])).astype(x.dtype)
    level = np.where(t > s, GLA_LEVELS - 1 - top, np.where(t == s, GLA_LEVELS, -1)).astype(np.int32)
    return table, level


def _layer_norm(x, g, b):
    mu = jnp.mean(x, axis=-1, keepdims=True)
    xc = x - mu
    var = jnp.mean(xc * xc, axis=-1, keepdims=True)
    return xc * jax.lax.rsqrt(var + LN_EPS) * g + b


def _gelu(x):
    hx = 0.5 * x
    return hx + hx * jnp.tanh(x * (GELU_C + (GELU_C * 0.044715) * (x * x)))


def _sigmoid(x):
    return 0.5 * jnp.tanh(0.5 * x) + 0.5


def _split_bf16(x):
    hi = x.astype(BF16)
    lo = (x - hi.astype(F32)).astype(BF16)
    return hi, lo


def _dot(a, b):
    return jnp.dot(a, b, preferred_element_type=F32)


def _dot_nt(a, b):
    return jax.lax.dot_general(a, b, (((1,), (1,)), ((), ())), preferred_element_type=F32)


def _dot_tn(a, b):
    return jax.lax.dot_general(a, b, (((0,), (0,)), ((), ())), preferred_element_type=F32)


def _cast_projection_weights(w_hbm, w_ref, stage_ref, sem):
    starts = [r for r in range(0, D_MAIN + GLA_GATE_RANK, W_STAGE_ROWS) if r < OFF_LOW]
    starts += list(range(OFF_LOW + GLA_GATE_RANK, D_MAIN + GLA_GATE_RANK, W_STAGE_ROWS))

    def copy(j):
        return pltpu.make_async_copy(w_hbm.at[pl.ds(starts[j], W_STAGE_ROWS)], stage_ref.at[j % 2], sem.at[j % 2])

    copy(0).start()
    for j, r0 in enumerate(starts):
        if j + 1 < len(starts):
            copy(j + 1).start()
        copy(j).wait()
        c0 = r0 if r0 < OFF_LOW else r0 - GLA_GATE_RANK
        w_ref[:, c0:c0 + W_STAGE_ROWS] = jnp.transpose(stage_ref[j % 2]).astype(BF16)


def _mixer_kernel(x_ref, w1f_ref, w2f_ref, w_hbm, b_ref, wlow_ref, blow_ref, wg_ref, bg_ref,
                  sg_g_ref, sg_b_ref, ws_ref, bs_ref, tab_ref, lvl_ref, upper_ref, gn_ref,
                  woutf_ref, ln_g_ref, ln_b_ref, o_ref, w1b_ref, w2b_ref,
                  state_ref, w_ref, wout_ref, stage_ref, sem, *, alpha):
    @pl.when((pl.program_id(0) == 0) & (pl.program_id(1) == 0))
    def _():
        _cast_projection_weights(w_hbm, w_ref, stage_ref, sem)
        wout_ref[...] = woutf_ref[...].astype(BF16)

    @pl.when(pl.program_id(1) == 0)
    def _():
        state_ref[...] = jnp.zeros_like(state_ref)

    w1b_ref[...] = w1f_ref[...].astype(BF16)
    w2b_ref[...] = w2f_ref[...].astype(BF16)

    x = x_ref[0]
    xh = x.astype(BF16)
    n_chunks = MIX_BLOCK // GLA_CHUNK

    def proj(off, width):
        c0 = off if off < OFF_LOW else off - GLA_GATE_RANK
        return _dot(xh, w_ref[:, c0:c0 + width]) + b_ref[:, c0:c0 + width]

    pv = proj(OFF_V, D_MODEL)
    q = proj(OFF_Q, GLA_KEY_DIM) * (GLA_HEAD_K ** -0.5)
    k = proj(OFF_K, GLA_KEY_DIM)
    a3 = (_dot_nt(xh, wlow_ref[...]) + blow_ref[...])[:, :LANES]

    v_sg = _layer_norm(_gelu(pv), sg_g_ref[...], sg_b_ref[...]).astype(BF16)
    a3_hi = a3.astype(BF16).astype(F32)
    lane = jax.lax.broadcasted_iota(jnp.int32, a3.shape, 1)
    mid = (lane >= GLA_GATE_RANK) & (lane < 2 * GLA_GATE_RANK)
    z = _dot(jnp.where(mid, a3 - a3_hi, a3_hi).astype(BF16), wg_ref[...]) + bg_ref[...]
    pu = proj(OFF_U, D_MODEL)
    log_f = (jnp.minimum(z, 0.0) - jnp.log(1.0 + jnp.exp(-jnp.abs(z)))) * (LOG2E / GLA_GATE_TEMP)

    row = jax.lax.broadcasted_iota(jnp.int32, (SG_CHUNK, SG_CHUNK), 0)
    col = jax.lax.broadcasted_iota(jnp.int32, (SG_CHUNK, SG_CHUNK), 1)
    mixed_g = []
    for g in range(SG_GROUPS):
        w_g = jnp.where(row >= col, ws_ref[g], 0.0).astype(BF16)
        gs = slice(g * LANES, (g + 1) * LANES)
        rhs = jnp.concatenate([v_sg[c * SG_CHUNK:(c + 1) * SG_CHUNK, gs] for c in range(MIX_BLOCK // SG_CHUNK)],
                              axis=1)
        mixed_g.append(_dot(w_g, rhs))
    mixed = jnp.concatenate(
        [jnp.concatenate([mg[:, c * LANES:(c + 1) * LANES] for mg in mixed_g], axis=1) + bs_ref[...]
         for c in range(MIX_BLOCK // SG_CHUNK)], axis=0)
    pga = proj(OFF_GA, D_MODEL)

    decays, prefixes, suffixes = [], [], []
    for c in range(n_chunks):
        f_hi, f_lo = _split_bf16(log_f[c * GLA_CHUNK:(c + 1) * GLA_CHUNK])
        expo = _dot(tab_ref[...], jnp.concatenate([f_hi, f_lo], axis=0))
        cum = expo[GLA_LEVELS * GLA_CHUNK:]
        decays.append(jnp.exp2(expo[:GLA_LEVELS * GLA_CHUNK]))
        prefixes.append(jnp.exp2(cum))
        suffixes.append(jnp.exp2(cum[GLA_CHUNK - 1:GLA_CHUNK] - cum))

    lvl2 = lvl_ref[...]
    klane = jax.lax.broadcasted_iota(jnp.int32, (GLA_CHUNK, GLA_KEY_DIM), 1)
    even_head = (klane // GLA_HEAD_K) % 2 == 0
    keep_even = jnp.where(even_head, 1.0, 0.0).astype(BF16)
    keep_odd = jnp.where(even_head, 0.0, 1.0).astype(BF16)
    pair_w = 2 * GLA_HEAD_K

    def pair_scores(lhs, rhs):
        bd = jnp.concatenate([rhs * keep_even, rhs * keep_odd], axis=0)
        return [_dot_nt(lhs[:, p * pair_w:(p + 1) * pair_w], bd[:, p * pair_w:(p + 1) * pair_w])
                for p in range(GLA_HEADS // 2)]

    def level_update(scores, c, lev):
        qc = q[c * GLA_CHUNK:(c + 1) * GLA_CHUNK]
        kc = k[c * GLA_CHUNK:(c + 1) * GLA_CHUNK]
        w = GLA_CHUNK >> (lev + 1)
        n_blk = GLA_CHUNK // w
        if w >= 8:
            mixed_qk = jnp.concatenate([(qc if (b % 2) else kc)[b * w:(b + 1) * w] for b in range(n_blk)], axis=0)
        else:
            mixed_qk = jnp.where(upper_ref[GLA_LEVELS - 1 - lev] != 0, qc, kc)
        zl = (mixed_qk * decays[c][lev * GLA_CHUNK:(lev + 1) * GLA_CHUNK]).astype(BF16)
        if w >= BF16_ROWS:
            lhs = jnp.concatenate([zl[b * w:(b + 1) * w] for b in range(1, n_blk, 2)], axis=0)
            parts = pair_scores(lhs, zl)
            out = []
            for p in range(GLA_HEADS // 2):
                rows = []
                for b in range(n_blk):
                    old = scores[p][b * w:(b + 1) * w]
                    if b % 2:
                        new = parts[p][(b // 2) * w:(b // 2 + 1) * w]
                        old = jnp.where(lvl2[b * w:(b + 1) * w] == lev, new, old)
                    rows.append(old)
                out.append(jnp.concatenate(rows, axis=0))
            return out
        parts = pair_scores(zl, zl)
        return [jnp.where(lvl2 == lev, parts[p], scores[p]) for p in range(GLA_HEADS // 2)]

    def diagonal_update(scores, c):
        qk = q[c * GLA_CHUNK:(c + 1) * GLA_CHUNK] * k[c * GLA_CHUNK:(c + 1) * GLA_CHUNK]
        out = []
        for p in range(GLA_HEADS // 2):
            diag = [jnp.broadcast_to(jnp.sum(qk[:, h * GLA_HEAD_K:(h + 1) * GLA_HEAD_K], axis=1, keepdims=True),
                                     (GLA_CHUNK, GLA_CHUNK)) for h in (2 * p, 2 * p + 1)]
            out.append(jnp.where(lvl2 == GLA_LEVELS, jnp.concatenate(diag, axis=1), scores[p]))
        return out

    all_scores = [[jnp.zeros((GLA_CHUNK, 2 * GLA_CHUNK), F32) for _ in range(GLA_HEADS // 2)]
                  for _ in range(n_chunks)]
    m_acc = _sigmoid(pga) * (_gelu(pu) * mixed)
    for lev in range(GLA_LEVELS):
        all_scores = [level_update(all_scores[c], c, lev) for c in range(n_chunks)]
        if lev == 1:
            gv = proj(OFF_GV, GLA_VAL_DIM).astype(BF16)
        if lev == 4:
            r = proj(OFF_R, GLA_VAL_DIM)
    all_scores = [diagonal_update(all_scores[c], c) for c in range(n_chunks)]

    o_rows = []
    for c in range(n_chunks):
        r0 = c * GLA_CHUNK
        q_in = (q[r0:r0 + GLA_CHUNK] * prefixes[c]).astype(BF16)
        k_out = (k[r0:r0 + GLA_CHUNK] * suffixes[c]).astype(BF16)
        total = prefixes[c][GLA_CHUNK - 1:GLA_CHUNK]
        heads = []
        for h in range(GLA_HEADS):
            sl = slice(h * GLA_HEAD_K, (h + 1) * GLA_HEAD_K)
            vh = gv[r0:r0 + GLA_CHUNK, h * GLA_HEAD_V:(h + 1) * GLA_HEAD_V]
            st = state_ref[h]
            s_h = all_scores[c][h // 2][:, (h % 2) * GLA_CHUNK:(h % 2 + 1) * GLA_CHUNK].astype(BF16)
            o_h = _dot(jnp.concatenate([s_h, q_in[:, sl]], axis=1),
                       jnp.concatenate([vh, st.astype(BF16)], axis=0))
            tcol = jnp.transpose(jnp.broadcast_to(total[:, sl], (GLA_HEAD_K, GLA_HEAD_K)))
            state_ref[h] = st * jnp.concatenate([tcol, tcol], axis=1) + _dot_tn(k_out[:, sl], vh)
            o_h = o_h * jax.lax.rsqrt(jnp.mean(o_h * o_h, axis=-1, keepdims=True) + LN_EPS)
            heads.append(o_h)
        o_rows.append(jnp.concatenate(heads, axis=1))
    pgb = proj(OFF_GB, D_MODEL)
    y_b = (jnp.concatenate(o_rows, axis=0) * gn_ref[...]) * (r * _sigmoid(r))
    m = (m_acc + _sigmoid(pgb) * y_b).astype(BF16)

    rows = MIX_BLOCK // MIX_TAIL_PIECES
    for r0 in range(0, MIX_BLOCK, rows):
        h1 = alpha * x[r0:r0 + rows] + _dot(m[r0:r0 + rows], wout_ref[...])
        o_ref[0, r0:r0 + rows, :] = _layer_norm(h1, ln_g_ref[...], ln_b_ref[...])


def _ffn_kernel(h_ref, w1_ref, w2_ref, ln_g_ref, ln_b_ref, o_ref, *, alpha):
    h = h_ref[...]
    hb = h.astype(BF16)
    acc = alpha * h
    n_ff = D_FF // D_MODEL
    for j in range(n_ff):
        a = jnp.maximum(_dot(hb, w1_ref[:, j * D_MODEL:(j + 1) * D_MODEL]), 0.0)
        a = (a * a).astype(BF16)
        w2_j = w2_ref[j * D_MODEL:(j + 1) * D_MODEL, :]
        if j + 1 < n_ff:
            acc = acc + _dot(a, w2_j)
        else:
            rows = FFN_BLOCK // FFN_TAIL_PIECES
            for r0 in range(0, FFN_BLOCK, rows):
                out = acc[r0:r0 + rows] + _dot(a[r0:r0 + rows], w2_j)
                o_ref[r0:r0 + rows, :] = _layer_norm(out, ln_g_ref[...], ln_b_ref[...])


def _resident(shape):
    nd = len(shape)
    return pl.BlockSpec(shape, lambda *_: (0,) * nd, pipeline_mode=pl.Buffered(1))


def _mixer_call(x, w1, w2, consts, alpha):
    bsz, t, d = x.shape
    blocks_per_row = t // MIX_BLOCK
    n_steps = bsz * blocks_per_row
    slab1, slab2 = w1.shape[0] // n_steps, w2.shape[0] // n_steps

    def slab(b, i):
        return (b * blocks_per_row + i, 0)

    return pl.pallas_call(
        functools.partial(_mixer_kernel, alpha=alpha),
        out_shape=(jax.ShapeDtypeStruct((bsz, t, d), F32),
                   jax.ShapeDtypeStruct(w1.shape, BF16), jax.ShapeDtypeStruct(w2.shape, BF16)),
        grid=(bsz, blocks_per_row),
        in_specs=[pl.BlockSpec((1, MIX_BLOCK, d), lambda b, i: (b, i, 0)),
                  pl.BlockSpec((slab1, w1.shape[1]), slab), pl.BlockSpec((slab2, w2.shape[1]), slab)]
        + [pl.BlockSpec(memory_space=pl.ANY)] + [_resident(c.shape) for c in consts[1:]],
        out_specs=(pl.BlockSpec((1, MIX_BLOCK, d), lambda b, i: (b, i, 0)),
                   pl.BlockSpec((slab1, w1.shape[1]), slab), pl.BlockSpec((slab2, w2.shape[1]), slab)),
        scratch_shapes=[pltpu.VMEM((GLA_HEADS, GLA_HEAD_K, GLA_HEAD_V), F32),
                        pltpu.VMEM((d, D_MAIN), BF16),
                        pltpu.VMEM((d, d), BF16),
                        pltpu.VMEM((2, W_STAGE_ROWS, d), F32),
                        pltpu.SemaphoreType.DMA((2,))],
        compiler_params=pltpu.CompilerParams(
            dimension_semantics=("arbitrary", "arbitrary"), vmem_limit_bytes=VMEM_LIMIT),
        name="mixer",
    )(x, w1, w2, *consts)


def _ffn_call(h, w1, w2, ln_g, ln_b, alpha):
    n, d = h.shape
    return pl.pallas_call(
        functools.partial(_ffn_kernel, alpha=alpha),
        out_shape=jax.ShapeDtypeStruct((n, d), F32),
        grid=(n // FFN_BLOCK,),
        in_specs=[pl.BlockSpec((FFN_BLOCK, d), lambda i: (i, 0)),
                  _resident(w1.shape), _resident(w2.shape), _resident(ln_g.shape), _resident(ln_b.shape)],
        out_specs=pl.BlockSpec((FFN_BLOCK, d), lambda i: (i, 0)),
        compiler_params=pltpu.CompilerParams(
            dimension_semantics=("arbitrary",), vmem_limit_bytes=VMEM_LIMIT),
        name="ffn",
    )(h, w1, w2, ln_g, ln_b)


def kernel(x, w_in, b_in, sg_ln_g, sg_ln_b, sg_w_s, sg_b_s, gla_w_gate2, gla_b_gate, gla_norm_g,
           w_out, ln1_g, ln1_b, w_ff1, w_ff2, ln2_g, ln2_b):
    bsz, t, d = x.shape
    depth = w_in.shape[0]
    alpha = (2.0 * depth) ** 0.25
    table_np, level_np = _decay_tables()
    table = jnp.asarray(np.concatenate([table_np, table_np], axis=1), BF16)
    level = jnp.asarray(np.concatenate([level_np, level_np], axis=1))
    upper = jnp.asarray(np.stack([np.broadcast_to(((np.arange(GLA_CHUNK) >> j) & 1)[:, None],
                                                  (GLA_CHUNK, GLA_KEY_DIM)) for j in range(3)]).astype(np.int32))
    low0, low1 = OFF_LOW, OFF_LOW + GLA_GATE_RANK
    rank = GLA_GATE_RANK
    h = x
    for l in range(depth):
        w_main = jnp.transpose(w_in[l])
        b_main = jnp.concatenate([b_in[l][:low0], b_in[l][low1:]])[None, :]
        w_low = jnp.pad(jnp.tile(w_main[low0:low1].astype(BF16), (3, 1)), ((0, 2 * LANES - 3 * rank), (0, 0)))
        b_low = jnp.pad(jnp.tile(b_in[l][low0:low1], 3), (0, 2 * LANES - 3 * rank))[None, :]
        wg_hi = gla_w_gate2[l].astype(BF16)
        wg_lo = (gla_w_gate2[l] - wg_hi.astype(F32)).astype(BF16)
        wg = jnp.pad(jnp.concatenate([wg_hi, wg_hi, wg_lo], axis=0), ((0, LANES - 3 * rank), (0, 0)))
        bs_plane = jnp.repeat(jnp.transpose(sg_b_s[l]), D_MODEL // SG_GROUPS, axis=1)
        consts = (w_main, b_main, w_low, b_low, wg, gla_b_gate[l][None, :],
                  sg_ln_g[l][None, :], sg_ln_b[l][None, :], sg_w_s[l], bs_plane, table, level, upper,
                  gla_norm_g[l][None, :], w_out[l], ln1_g[l][None, :], ln1_b[l][None, :])
        h, w1b, w2b = _mixer_call(h, w_ff1[l], w_ff2[l], consts, alpha)
        h = _ffn_call(h.reshape(bsz * t, d), w1b, w2b,
                      ln2_g[l][None, :], ln2_b[l][None, :], alpha).reshape(bsz, t, d)
    return h
```

```python
import functools
import math

import jax
import jax.numpy as jnp
import numpy as np
from jax.experimental import pallas as pl
from jax.experimental.pallas import tpu as pltpu

D_MODEL = 1024
SG_CHUNK = 128
SG_GROUPS = 8
GLA_HEADS = 4
GLA_KEY_DIM = 512
GLA_VAL_DIM = 1024
GLA_HEAD_K = GLA_KEY_DIM // GLA_HEADS
GLA_HEAD_V = GLA_VAL_DIM // GLA_HEADS
GLA_GATE_RANK = 16
GLA_GATE_TEMP = 16.0
D_FF = 4 * D_MODEL
LN_EPS = 1e-5

LANES = 128
BF16_ROWS = 16
GLA_CHUNK = 128
GLA_LEVELS = int(math.log2(GLA_CHUNK))
MIX_BLOCK = 512
FFN_BLOCK = 1024
MIX_TAIL_PIECES = 2
FFN_TAIL_PIECES = 4
W_STAGE_ROWS = 512
VMEM_LIMIT = 56 * 1024 * 1024

OFF_U, OFF_V, OFF_Q, OFF_K, OFF_GV, OFF_R, OFF_LOW, OFF_GA, OFF_GB = 0, 1024, 2048, 2560, 3072, 4096, 5120, 5136, 6160
D_MAIN = 7168

GELU_C = 0.7978845608028654
LOG2E = 1.4426950408889634

BF16 = jnp.bfloat16
F32 = jnp.float32


def _decay_tables():
    c = GLA_CHUNK
    blocks = []
    for lw in range(GLA_LEVELS - 1, -1, -1):
        a = np.zeros((c, c), np.float32)
        for t in range(c):
            if (t >> lw) & 1:
                a[t, (t >> lw) << lw:t + 1] = 1.0
            else:
                a[t, t + 1:((t >> lw) + 1) << lw] = 1.0
        blocks.append(a)
    blocks.append(np.tril(np.ones((c, c), np.float32)))
    table = np.concatenate(blocks, axis=0)
    t = np.arange(c)[:, None]
    s = np.arange(c)[None, :]
    x = t ^ s
    top = np.zeros_like(x)
    nz = x > 0
    top[nz] = np.floor(np.log2(x[nz])).astype(x.dtype)
    level = np.where(t > s, GLA_LEVELS - 1 - top, np.where(t == s, GLA_LEVELS, -1)).astype(np.int32)
    return table, level


def _layer_norm(x, g, b):
    mu = jnp.mean(x, axis=-1, keepdims=True)
    xc = x - mu
    var = jnp.mean(xc * xc, axis=-1, keepdims=True)
    return xc * jax.lax.rsqrt(var + LN_EPS) * g + b


def _gelu(x):
    hx = 0.5 * x
    return hx + hx * jnp.tanh(x * (GELU_C + (GELU_C * 0.044715) * (x * x)))


def _sigmoid(x):
    return 0.5 * jnp.tanh(0.5 * x) + 0.5


def _split_bf16(x):
    hi = x.astype(BF16)
    lo = (x - hi.astype(F32)).astype(BF16)
    return hi, lo


def _dot(a, b):
    return jnp.dot(a, b, preferred_element_type=F32)


def _dot_nt(a, b):
    return jax.lax.dot_general(a, b, (((1,), (1,)), ((), ())), preferred_element_type=F32)


def _dot_tn(a, b):
    return jax.lax.dot_general(a, b, (((0,), (0,)), ((), ())), preferred_element_type=F32)


def _cast_projection_weights(w_hbm, w_ref, stage_ref, sem):
    starts = [r for r in range(0, D_MAIN + GLA_GATE_RANK, W_STAGE_ROWS) if r < OFF_LOW]
    starts += list(range(OFF_LOW + GLA_GATE_RANK, D_MAIN + GLA_GATE_RANK, W_STAGE_ROWS))

    def copy(j):
        return pltpu.make_async_copy(w_hbm.at[pl.ds(starts[j], W_STAGE_ROWS)], stage_ref.at[j % 2], sem.at[j % 2])

    copy(0).start()
    for j, r0 in enumerate(starts):
        if j + 1 < len(starts):
            copy(j + 1).start()
        copy(j).wait()
        c0 = r0 if r0 < OFF_LOW else r0 - GLA_GATE_RANK
        w_ref[:, c0:c0 + W_STAGE_ROWS] = jnp.transpose(stage_ref[j % 2]).astype(BF16)


def _mixer_kernel(x_ref, w1f_ref, w2f_ref, w_hbm, b_ref, wlow_ref, blow_ref, wg_ref, bg_ref,
                  sg_g_ref, sg_b_ref, ws_ref, bs_ref, tab_ref, lvl_ref, upper_ref, gn_ref,
                  woutf_ref, ln_g_ref, ln_b_ref, o_ref, w1b_ref, w2b_ref,
                  state_ref, w_ref, wout_ref, stage_ref, sem, *, alpha):
    @pl.when((pl.program_id(0) == 0) & (pl.program_id(1) == 0))
    def _():
        _cast_projection_weights(w_hbm, w_ref, stage_ref, sem)
        wout_ref[...] = woutf_ref[...].astype(BF16)

    @pl.when(pl.program_id(1) == 0)
    def _():
        state_ref[...] = jnp.zeros_like(state_ref)

    w1b_ref[...] = w1f_ref[...].astype(BF16)
    w2b_ref[...] = w2f_ref[...].astype(BF16)

    x = x_ref[0]
    xh = x.astype(BF16)
    n_chunks = MIX_BLOCK // GLA_CHUNK

    def proj(off, width):
        c0 = off if off < OFF_LOW else off - GLA_GATE_RANK
        return _dot(xh, w_ref[:, c0:c0 + width]) + b_ref[:, c0:c0 + width]

    pv = proj(OFF_V, D_MODEL)
    q = proj(OFF_Q, GLA_KEY_DIM) * (GLA_HEAD_K ** -0.5)
    k = proj(OFF_K, GLA_KEY_DIM)
    a3 = (_dot_nt(xh, wlow_ref[...]) + blow_ref[...])[:, :LANES]

    v_sg = _layer_norm(_gelu(pv), sg_g_ref[...], sg_b_ref[...]).astype(BF16)
    a3_hi = a3.astype(BF16).astype(F32)
    lane = jax.lax.broadcasted_iota(jnp.int32, a3.shape, 1)
    mid = (lane >= GLA_GATE_RANK) & (lane < 2 * GLA_GATE_RANK)
    z = _dot(jnp.where(mid, a3 - a3_hi, a3_hi).astype(BF16), wg_ref[...]) + bg_ref[...]
    pu = proj(OFF_U, D_MODEL)
    log_f = (jnp.minimum(z, 0.0) - jnp.log(1.0 + jnp.exp(-jnp.abs(z)))) * (LOG2E / GLA_GATE_TEMP)

    row = jax.lax.broadcasted_iota(jnp.int32, (SG_CHUNK, SG_CHUNK), 0)
    col = jax.lax.broadcasted_iota(jnp.int32, (SG_CHUNK, SG_CHUNK), 1)
    mixed_g = []
    for g in range(SG_GROUPS):
        w_g = jnp.where(row >= col, ws_ref[g], 0.0).astype(BF16)
        gs = slice(g * LANES, (g + 1) * LANES)
        rhs = jnp.concatenate([v_sg[c * SG_CHUNK:(c + 1) * SG_CHUNK, gs] for c in range(MIX_BLOCK // SG_CHUNK)],
                              axis=1)
        mixed_g.append(_dot(w_g, rhs))
    mixed = jnp.concatenate(
        [jnp.concatenate([mg[:, c * LANES:(c + 1) * LANES] for mg in mixed_g], axis=1) + bs_ref[...]
         for c in range(MIX_BLOCK // SG_CHUNK)], axis=0)
    pga = proj(OFF_GA, D_MODEL)

    decays, prefixes, suffixes = [], [], []
    for c in range(n_chunks):
        f_hi, f_lo = _split_bf16(log_f[c * GLA_CHUNK:(c + 1) * GLA_CHUNK])
        expo = _dot(tab_ref[...], jnp.concatenate([f_hi, f_lo], axis=0))
        cum = expo[GLA_LEVELS * GLA_CHUNK:]
        decays.append(jnp.exp2(expo[:GLA_LEVELS * GLA_CHUNK]))
        prefixes.append(jnp.exp2(cum))
        suffixes.append(jnp.exp2(cum[GLA_CHUNK - 1:GLA_CHUNK] - cum))

    lvl2 = lvl_ref[...]
    klane = jax.lax.broadcasted_iota(jnp.int32, (GLA_CHUNK, GLA_KEY_DIM), 1)
    even_head = (klane // GLA_HEAD_K) % 2 == 0
    keep_even = jnp.where(even_head, 1.0, 0.0).astype(BF16)
    keep_odd = jnp.where(even_head, 0.0, 1.0).astype(BF16)
    pair_w = 2 * GLA_HEAD_K

    def pair_scores(lhs, rhs):
        bd = jnp.concatenate([rhs * keep_even, rhs * keep_odd], axis=0)
        return [_dot_nt(lhs[:, p * pair_w:(p + 1) * pair_w], bd[:, p * pair_w:(p + 1) * pair_w])
                for p in range(GLA_HEADS // 2)]

    def level_update(scores, c, lev):
        qc = q[c * GLA_CHUNK:(c + 1) * GLA_CHUNK]
        kc = k[c * GLA_CHUNK:(c + 1) * GLA_CHUNK]
        w = GLA_CHUNK >> (lev + 1)
        n_blk = GLA_CHUNK // w
        if w >= 8:
            mixed_qk = jnp.concatenate([(qc if (b % 2) else kc)[b * w:(b + 1) * w] for b in range(n_blk)], axis=0)
        else:
            mixed_qk = jnp.where(upper_ref[GLA_LEVELS - 1 - lev] != 0, qc, kc)
        zl = (mixed_qk * decays[c][lev * GLA_CHUNK:(lev + 1) * GLA_CHUNK]).astype(BF16)
        if w >= BF16_ROWS:
            lhs = jnp.concatenate([zl[b * w:(b + 1) * w] for b in range(1, n_blk, 2)], axis=0)
            parts = pair_scores(lhs, zl)
            out = []
            for p in range(GLA_HEADS // 2):
                rows = []
                for b in range(n_blk):
                    old = scores[p][b * w:(b + 1) * w]
                    if b % 2:
                        new = parts[p][(b // 2) * w:(b // 2 + 1) * w]
                        old = jnp.where(lvl2[b * w:(b + 1) * w] == lev, new, old)
                    rows.append(old)
                out.append(jnp.concatenate(rows, axis=0))
            return out
        parts = pair_scores(zl, zl)
        return [jnp.where(lvl2 == lev, parts[p], scores[p]) for p in range(GLA_HEADS // 2)]

    def diagonal_update(scores, c):
        qk = q[c * GLA_CHUNK:(c + 1) * GLA_CHUNK] * k[c * GLA_CHUNK:(c + 1) * GLA_CHUNK]
        out = []
        for p in range(GLA_HEADS // 2):
            diag = [jnp.broadcast_to(jnp.sum(qk[:, h * GLA_HEAD_K:(h + 1) * GLA_HEAD_K], axis=1, keepdims=True),
                                     (GLA_CHUNK, GLA_CHUNK)) for h in (2 * p, 2 * p + 1)]
            out.append(jnp.where(lvl2 == GLA_LEVELS, jnp.concatenate(diag, axis=1), scores[p]))
        return out

    all_scores = [[jnp.zeros((GLA_CHUNK, 2 * GLA_CHUNK), F32) for _ in range(GLA_HEADS // 2)]
                  for _ in range(n_chunks)]
    m_acc = _sigmoid(pga) * (_gelu(pu) * mixed)
    for lev in range(GLA_LEVELS):
        all_scores = [level_update(all_scores[c], c, lev) for c in range(n_chunks)]
        if lev == 1:
            gv = proj(OFF_GV, GLA_VAL_DIM).astype(BF16)
        if lev == 4:
            r = proj(OFF_R, GLA_VAL_DIM)
    all_scores = [diagonal_update(all_scores[c], c) for c in range(n_chunks)]

    heads_sl = [slice(h * GLA_HEAD_K, (h + 1) * GLA_HEAD_K) for h in range(GLA_HEADS)]
    v_heads = [[gv[c * GLA_CHUNK:(c + 1) * GLA_CHUNK, h * GLA_HEAD_V:(h + 1) * GLA_HEAD_V]
                for h in range(GLA_HEADS)] for c in range(n_chunks)]
    kv, scale, q_ins = [], [], []
    for c in range(n_chunks):
        r0 = c * GLA_CHUNK
        q_ins.append((q[r0:r0 + GLA_CHUNK] * prefixes[c]).astype(BF16))
        k_out = (k[r0:r0 + GLA_CHUNK] * suffixes[c]).astype(BF16)
        total = prefixes[c][GLA_CHUNK - 1:GLA_CHUNK]
        kv.append([_dot_tn(k_out[:, sl], v_heads[c][h]) for h, sl in enumerate(heads_sl)])
        tcols = [jnp.transpose(jnp.broadcast_to(total[:, sl], (GLA_HEAD_K, GLA_HEAD_K))) for sl in heads_sl]
        scale.append([jnp.concatenate([t, t], axis=1) for t in tcols])
    states = [state_ref[h] for h in range(GLA_HEADS)]
    o_rows = []
    for c in range(n_chunks):
        heads = []
        for h, sl in enumerate(heads_sl):
            s_h = all_scores[c][h // 2][:, (h % 2) * GLA_CHUNK:(h % 2 + 1) * GLA_CHUNK].astype(BF16)
            o_h = _dot(jnp.concatenate([s_h, q_ins[c][:, sl]], axis=1),
                       jnp.concatenate([v_heads[c][h], states[h].astype(BF16)], axis=0))
            states[h] = states[h] * scale[c][h] + kv[c][h]
            o_h = o_h * jax.lax.rsqrt(jnp.mean(o_h * o_h, axis=-1, keepdims=True) + LN_EPS)
            heads.append(o_h)
        o_rows.append(jnp.concatenate(heads, axis=1))
    for h in range(GLA_HEADS):
        state_ref[h] = states[h]
    pgb = proj(OFF_GB, D_MODEL)
    y_b = (jnp.concatenate(o_rows, axis=0) * gn_ref[...]) * (r * _sigmoid(r))
    m = (m_acc + _sigmoid(pgb) * y_b).astype(BF16)

    rows = MIX_BLOCK // MIX_TAIL_PIECES
    for r0 in range(0, MIX_BLOCK, rows):
        h1 = alpha * x[r0:r0 + rows] + _dot(m[r0:r0 + rows], wout_ref[...])
        o_ref[0, r0:r0 + rows, :] = _layer_norm(h1, ln_g_ref[...], ln_b_ref[...])


def _ffn_kernel(h_ref, w1_ref, w2_ref, ln_g_ref, ln_b_ref, o_ref, *, alpha):
    h = h_ref[...]
    hb = h.astype(BF16)
    acc = alpha * h
    n_ff = D_FF // D_MODEL
    for j in range(n_ff):
        a = jnp.maximum(_dot(hb, w1_ref[:, j * D_MODEL:(j + 1) * D_MODEL]), 0.0)
        a = (a * a).astype(BF16)
        w2_j = w2_ref[j * D_MODEL:(j + 1) * D_MODEL, :]
        if j + 1 < n_ff:
            acc = acc + _dot(a, w2_j)
        else:
            rows = FFN_BLOCK // FFN_TAIL_PIECES
            for r0 in range(0, FFN_BLOCK, rows):
                out = acc[r0:r0 + rows] + _dot(a[r0:r0 + rows], w2_j)
                o_ref[r0:r0 + rows, :] = _layer_norm(out, ln_g_ref[...], ln_b_ref[...])


def _resident(shape):
    nd = len(shape)
    return pl.BlockSpec(shape, lambda *_: (0,) * nd, pipeline_mode=pl.Buffered(1))


def _mixer_call(x, w1, w2, consts, alpha):
    bsz, t, d = x.shape
    blocks_per_row = t // MIX_BLOCK
    n_steps = bsz * blocks_per_row
    slab1, slab2 = w1.shape[0] // n_steps, w2.shape[0] // n_steps

    def slab(b, i):
        return (b * blocks_per_row + i, 0)

    return pl.pallas_call(
        functools.partial(_mixer_kernel, alpha=alpha),
        out_shape=(jax.ShapeDtypeStruct((bsz, t, d), F32),
                   jax.ShapeDtypeStruct(w1.shape, BF16), jax.ShapeDtypeStruct(w2.shape, BF16)),
        grid=(bsz, blocks_per_row),
        in_specs=[pl.BlockSpec((1, MIX_BLOCK, d), lambda b, i: (b, i, 0)),
                  pl.BlockSpec((slab1, w1.shape[1]), slab), pl.BlockSpec((slab2, w2.shape[1]), slab)]
        + [pl.BlockSpec(memory_space=pl.ANY)] + [_resident(c.shape) for c in consts[1:]],
        out_specs=(pl.BlockSpec((1, MIX_BLOCK, d), lambda b, i: (b, i, 0)),
                   pl.BlockSpec((slab1, w1.shape[1]), slab), pl.BlockSpec((slab2, w2.shape[1]), slab)),
        scratch_shapes=[pltpu.VMEM((GLA_HEADS, GLA_HEAD_K, GLA_HEAD_V), F32),
                        pltpu.VMEM((d, D_MAIN), BF16),
                        pltpu.VMEM((d, d), BF16),
                        pltpu.VMEM((2, W_STAGE_ROWS, d), F32),
                        pltpu.SemaphoreType.DMA((2,))],
        compiler_params=pltpu.CompilerParams(
            dimension_semantics=("arbitrary", "arbitrary"), vmem_limit_bytes=VMEM_LIMIT),
        name="mixer",
    )(x, w1, w2, *consts)


def _ffn_call(h, w1, w2, ln_g, ln_b, alpha):
    n, d = h.shape
    return pl.pallas_call(
        functools.partial(_ffn_kernel, alpha=alpha),
        out_shape=jax.ShapeDtypeStruct((n, d), F32),
        grid=(n // FFN_BLOCK,),
        in_specs=[pl.BlockSpec((FFN_BLOCK, d), lambda i: (i, 0)),
                  _resident(w1.shape), _resident(w2.shape), _resident(ln_g.shape), _resident(ln_b.shape)],
        out_specs=pl.BlockSpec((FFN_BLOCK, d), lambda i: (i, 0)),
        compiler_params=pltpu.CompilerParams(
            dimension_semantics=("arbitrary",), vmem_limit_bytes=VMEM_LIMIT),
        name="ffn",
    )(h, w1, w2, ln_g, ln_b)


def kernel(x, w_in, b_in, sg_ln_g, sg_ln_b, sg_w_s, sg_b_s, gla_w_gate2, gla_b_gate, gla_norm_g,
           w_out, ln1_g, ln1_b, w_ff1, w_ff2, ln2_g, ln2_b):
    bsz, t, d = x.shape
    depth = w_in.shape[0]
    alpha = (2.0 * depth) ** 0.25
    table_np, level_np = _decay_tables()
    table = jnp.asarray(np.concatenate([table_np, table_np], axis=1), BF16)
    level = jnp.asarray(np.concatenate([level_np, level_np], axis=1))
    upper = jnp.asarray(np.stack([np.broadcast_to(((np.arange(GLA_CHUNK) >> j) & 1)[:, None],
                                                  (GLA_CHUNK, GLA_KEY_DIM)) for j in range(3)]).astype(np.int32))
    low0, low1 = OFF_LOW, OFF_LOW + GLA_GATE_RANK
    rank = GLA_GATE_RANK
    h = x
    for l in range(depth):
        w_main = jnp.transpose(w_in[l])
        b_main = jnp.concatenate([b_in[l][:low0], b_in[l][low1:]])[None, :]
        w_low = jnp.pad(jnp.tile(w_main[low0:low1].astype(BF16), (3, 1)), ((0, 2 * LANES - 3 * rank), (0, 0)))
        b_low = jnp.pad(jnp.tile(b_in[l][low0:low1], 3), (0, 2 * LANES - 3 * rank))[None, :]
        wg_hi = gla_w_gate2[l].astype(BF16)
        wg_lo = (gla_w_gate2[l] - wg_hi.astype(F32)).astype(BF16)
        wg = jnp.pad(jnp.concatenate([wg_hi, wg_hi, wg_lo], axis=0), ((0, LANES - 3 * rank), (0, 0)))
        bs_plane = jnp.repeat(jnp.transpose(sg_b_s[l]), D_MODEL // SG_GROUPS, axis=1)
        consts = (w_main, b_main, w_low, b_low, wg, gla_b_gate[l][None, :],
                  sg_ln_g[l][None, :], sg_ln_b[l][None, :], sg_w_s[l], bs_plane, table, level, upper,
                  gla_norm_g[l][None, :], w_out[l], ln1_g[l][None, :], ln1_b[l][None, :])
        h, w1b, w2b = _mixer_call(h, w_ff1[l], w_ff2[l], consts, alpha)
        h = _ffn_call(h.reshape(bsz * t, d), w1b, w2b,
                      ln2_g[l][None, :], ln2_b[l][None, :], alpha).reshape(bsz, t, d)
    return h
```

```python
import functools
import math

import jax
import jax.numpy as jnp
import numpy as np
from jax.experimental import pallas as pl
from jax.experimental.pallas import tpu as pltpu

D_MODEL = 1024
SG_CHUNK = 128
SG_GROUPS = 8
GLA_HEADS = 4
GLA_KEY_DIM = 512
GLA_VAL_DIM = 1024
GLA_HEAD_K = GLA_KEY_DIM // GLA_HEADS
GLA_HEAD_V = GLA_VAL_DIM // GLA_HEADS
GLA_GATE_RANK = 16
GLA_GATE_TEMP = 16.0
D_FF = 4 * D_MODEL
LN_EPS = 1e-5

LANES = 128
BF16_ROWS = 16
GLA_CHUNK = 128
GLA_LEVELS = int(math.log2(GLA_CHUNK))
MIX_BLOCK = 512
FFN_BLOCK = 1024
MIX_TAIL_PIECES = 2
FFN_TAIL_PIECES = 4
W_STAGE_ROWS = 512
VMEM_LIMIT = 56 * 1024 * 1024

OFF_U, OFF_V, OFF_Q, OFF_K, OFF_GV, OFF_R, OFF_LOW, OFF_GA, OFF_GB = 0, 1024, 2048, 2560, 3072, 4096, 5120, 5136, 6160
D_MAIN = 7168

GELU_C = 0.7978845608028654
LOG2E = 1.4426950408889634

BF16 = jnp.bfloat16
F32 = jnp.float32


def _decay_tables():
    c = GLA_CHUNK
    blocks = []
    for lw in range(GLA_LEVELS - 1, -1, -1):
        a = np.zeros((c, c), np.float32)
        for t in range(c):
            if (t >> lw) & 1:
                a[t, (t >> lw) << lw:t + 1] = 1.0
            else:
                a[t, t + 1:((t >> lw) + 1) << lw] = 1.0
        blocks.append(a)
    blocks.append(np.tril(np.ones((c, c), np.float32)))
    table = np.concatenate(blocks, axis=0)
    t = np.arange(c)[:, None]
    s = np.arange(c)[None, :]
    x = t ^ s
    top = np.zeros_like(x)
    nz = x > 0
    top[nz] = np.floor(np.log2(x[nz])).astype(x.dtype)
    level = np.where(t > s, GLA_LEVELS - 1 - top, np.where(t == s, GLA_LEVELS, -1)).astype(np.int32)
    return table, level


def _layer_norm(x, g, b):
    mu = jnp.mean(x, axis=-1, keepdims=True)
    xc = x - mu
    var = jnp.mean(xc * xc, axis=-1, keepdims=True)
    return xc * jax.lax.rsqrt(var + LN_EPS) * g + b


def _gelu(x):
    hx = 0.5 * x
    return hx + hx * jnp.tanh(x * (GELU_C + (GELU_C * 0.044715) * (x * x)))


def _sigmoid(x):
    return 0.5 * jnp.tanh(0.5 * x) + 0.5


def _split_bf16(x):
    hi = x.astype(BF16)
    lo = (x - hi.astype(F32)).astype(BF16)
    return hi, lo


def _dot(a, b):
    return jnp.dot(a, b, preferred_element_type=F32)


def _dot_tn(a, b):
    return jax.lax.dot_general(a, b, (((0,), (0,)), ((), ())), preferred_element_type=F32)


def _cast_projection_weights(w_hbm, w_ref, stage_ref, sem):
    starts = [r for r in range(0, D_MAIN + GLA_GATE_RANK, W_STAGE_ROWS) if r < OFF_LOW]
    starts += list(range(OFF_LOW + GLA_GATE_RANK, D_MAIN + GLA_GATE_RANK, W_STAGE_ROWS))

    def copy(j):
        return pltpu.make_async_copy(w_hbm.at[pl.ds(starts[j], W_STAGE_ROWS)], stage_ref.at[j % 2], sem.at[j % 2])

    copy(0).start()
    for j, r0 in enumerate(starts):
        if j + 1 < len(starts):
            copy(j + 1).start()
        copy(j).wait()
        c0 = r0 if r0 < OFF_LOW else r0 - GLA_GATE_RANK
        w_ref[:, c0:c0 + W_STAGE_ROWS] = jnp.transpose(stage_ref[j % 2]).astype(BF16)


def _mixer_kernel(x_ref, w1f_ref, w2f_ref, w_hbm, b_ref, wlow_ref, blow_ref, wg_ref, bg_ref,
                  sg_g_ref, sg_b_ref, ws_ref, bs_ref, tab_ref, lvl_ref, upper_ref, gn_ref,
                  woutf_ref, ln_g_ref, ln_b_ref, o_ref, w1b_ref, w2b_ref,
                  state_ref, w_ref, wout_ref, stage_ref, sem, *, alpha):
    @pl.when((pl.program_id(0) == 0) & (pl.program_id(1) == 0))
    def _():
        _cast_projection_weights(w_hbm, w_ref, stage_ref, sem)
        wout_ref[...] = woutf_ref[...].astype(BF16)

    @pl.when(pl.program_id(1) == 0)
    def _():
        state_ref[...] = jnp.zeros_like(state_ref)

    w1b_ref[...] = w1f_ref[...].astype(BF16)
    w2b_ref[...] = w2f_ref[...].astype(BF16)

    x = x_ref[0]
    xh = x.astype(BF16)
    n_chunks = MIX_BLOCK // GLA_CHUNK

    def proj(off, width):
        c0 = off if off < OFF_LOW else off - GLA_GATE_RANK
        return _dot(xh, w_ref[:, c0:c0 + width]) + b_ref[:, c0:c0 + width]

    pv = proj(OFF_V, D_MODEL)
    q = proj(OFF_Q, GLA_KEY_DIM) * (GLA_HEAD_K ** -0.5)
    k = proj(OFF_K, GLA_KEY_DIM)
    a3 = (_dot(xh, wlow_ref[...]) + blow_ref[...])[:, :LANES]

    v_sg = _layer_norm(_gelu(pv), sg_g_ref[...], sg_b_ref[...]).astype(BF16)
    a3_hi = a3.astype(BF16).astype(F32)
    lane = jax.lax.broadcasted_iota(jnp.int32, a3.shape, 1)
    mid = (lane >= GLA_GATE_RANK) & (lane < 2 * GLA_GATE_RANK)
    z = _dot(jnp.where(mid, a3 - a3_hi, a3_hi).astype(BF16), wg_ref[...]) + bg_ref[...]
    pu = proj(OFF_U, D_MODEL)
    log_f = (jnp.minimum(z, 0.0) - jnp.log(1.0 + jnp.exp(-jnp.abs(z)))) * (LOG2E / GLA_GATE_TEMP)

    row = jax.lax.broadcasted_iota(jnp.int32, (SG_CHUNK, SG_CHUNK), 0)
    col = jax.lax.broadcasted_iota(jnp.int32, (SG_CHUNK, SG_CHUNK), 1)
    mixed_g = []
    for g in range(SG_GROUPS):
        w_g = jnp.where(row >= col, ws_ref[g], 0.0).astype(BF16)
        gs = slice(g * LANES, (g + 1) * LANES)
        rhs = jnp.concatenate([v_sg[c * SG_CHUNK:(c + 1) * SG_CHUNK, gs] for c in range(MIX_BLOCK // SG_CHUNK)],
                              axis=1)
        mixed_g.append(_dot(w_g, rhs))
    mixed = jnp.concatenate(
        [jnp.concatenate([mg[:, c * LANES:(c + 1) * LANES] for mg in mixed_g], axis=1) + bs_ref[...]
         for c in range(MIX_BLOCK // SG_CHUNK)], axis=0)
    pga = proj(OFF_GA, D_MODEL)

    decays, prefixes, suffixes = [], [], []
    for c in range(n_chunks):
        f_hi, f_lo = _split_bf16(log_f[c * GLA_CHUNK:(c + 1) * GLA_CHUNK])
        expo = _dot(tab_ref[...], jnp.concatenate([f_hi, f_lo], axis=0))
        cum = expo[GLA_LEVELS * GLA_CHUNK:]
        decays.append(jnp.exp2(expo[:GLA_LEVELS * GLA_CHUNK]))
        prefixes.append(jnp.exp2(cum))
        suffixes.append(jnp.exp2(cum[GLA_CHUNK - 1:GLA_CHUNK] - cum))

    lvl2 = lvl_ref[...]
    pair_w = 2 * GLA_HEAD_K
    no_keys = jnp.zeros((GLA_HEAD_K, GLA_CHUNK), BF16)

    def pair_scores(lhs, keys_t):
        out = []
        for p in range(GLA_HEADS // 2):
            even = keys_t[(2 * p) * GLA_HEAD_K:(2 * p + 1) * GLA_HEAD_K]
            odd = keys_t[(2 * p + 1) * GLA_HEAD_K:(2 * p + 2) * GLA_HEAD_K]
            bd = jnp.concatenate([jnp.concatenate([even, no_keys], axis=1),
                                  jnp.concatenate([no_keys, odd], axis=1)], axis=0)
            out.append(_dot(lhs[:, p * pair_w:(p + 1) * pair_w], bd))
        return out

    def level_update(scores, c, lev):
        qc = q[c * GLA_CHUNK:(c + 1) * GLA_CHUNK]
        kc = k[c * GLA_CHUNK:(c + 1) * GLA_CHUNK]
        w = GLA_CHUNK >> (lev + 1)
        n_blk = GLA_CHUNK // w
        if w >= 8:
            mixed_qk = jnp.concatenate([(qc if (b % 2) else kc)[b * w:(b + 1) * w] for b in range(n_blk)], axis=0)
        else:
            mixed_qk = jnp.where(upper_ref[GLA_LEVELS - 1 - lev] != 0, qc, kc)
        zl_f = mixed_qk * decays[c][lev * GLA_CHUNK:(lev + 1) * GLA_CHUNK]
        zl = zl_f.astype(BF16)
        zl_t = jnp.transpose(zl_f).astype(BF16)
        if w >= BF16_ROWS:
            lhs = jnp.concatenate([zl[b * w:(b + 1) * w] for b in range(1, n_blk, 2)], axis=0)
            parts = pair_scores(lhs, zl_t)
            out = []
            for p in range(GLA_HEADS // 2):
                rows = []
                for b in range(n_blk):
                    old = scores[p][b * w:(b + 1) * w]
                    if b % 2:
                        new = parts[p][(b // 2) * w:(b // 2 + 1) * w]
                        old = jnp.where(lvl2[b * w:(b + 1) * w] == lev, new, old)
                    rows.append(old)
                out.append(jnp.concatenate(rows, axis=0))
            return out
        parts = pair_scores(zl, zl_t)
        return [jnp.where(lvl2 == lev, parts[p], scores[p]) for p in range(GLA_HEADS // 2)]

    def diagonal_update(scores, c):
        qk = q[c * GLA_CHUNK:(c + 1) * GLA_CHUNK] * k[c * GLA_CHUNK:(c + 1) * GLA_CHUNK]
        out = []
        for p in range(GLA_HEADS // 2):
            diag = [jnp.broadcast_to(jnp.sum(qk[:, h * GLA_HEAD_K:(h + 1) * GLA_HEAD_K], axis=1, keepdims=True),
                                     (GLA_CHUNK, GLA_CHUNK)) for h in (2 * p, 2 * p + 1)]
            out.append(jnp.where(lvl2 == GLA_LEVELS, jnp.concatenate(diag, axis=1), scores[p]))
        return out

    all_scores = [[jnp.zeros((GLA_CHUNK, 2 * GLA_CHUNK), F32) for _ in range(GLA_HEADS // 2)]
                  for _ in range(n_chunks)]
    m_acc = _sigmoid(pga) * (_gelu(pu) * mixed)
    for lev in range(GLA_LEVELS):
        all_scores = [level_update(all_scores[c], c, lev) for c in range(n_chunks)]
        if lev == 1:
            gv = proj(OFF_GV, GLA_VAL_DIM).astype(BF16)
        if lev == 4:
            r = proj(OFF_R, GLA_VAL_DIM)
    all_scores = [diagonal_update(all_scores[c], c) for c in range(n_chunks)]

    heads_sl = [slice(h * GLA_HEAD_K, (h + 1) * GLA_HEAD_K) for h in range(GLA_HEADS)]
    v_heads = [[gv[c * GLA_CHUNK:(c + 1) * GLA_CHUNK, h * GLA_HEAD_V:(h + 1) * GLA_HEAD_V]
                for h in range(GLA_HEADS)] for c in range(n_chunks)]
    kv, scale, q_ins = [], [], []
    for c in range(n_chunks):
        r0 = c * GLA_CHUNK
        q_ins.append((q[r0:r0 + GLA_CHUNK] * prefixes[c]).astype(BF16))
        k_out = (k[r0:r0 + GLA_CHUNK] * suffixes[c]).astype(BF16)
        total = prefixes[c][GLA_CHUNK - 1:GLA_CHUNK]
        kv.append([_dot_tn(k_out[:, sl], v_heads[c][h]) for h, sl in enumerate(heads_sl)])
        tcols = [jnp.transpose(jnp.broadcast_to(total[:, sl], (GLA_HEAD_K, GLA_HEAD_K))) for sl in heads_sl]
        scale.append([jnp.concatenate([t, t], axis=1) for t in tcols])
    states = [state_ref[h] for h in range(GLA_HEADS)]
    o_rows = []
    for c in range(n_chunks):
        heads = []
        for h, sl in enumerate(heads_sl):
            s_h = all_scores[c][h // 2][:, (h % 2) * GLA_CHUNK:(h % 2 + 1) * GLA_CHUNK].astype(BF16)
            o_h = _dot(jnp.concatenate([s_h, q_ins[c][:, sl]], axis=1),
                       jnp.concatenate([v_heads[c][h], states[h].astype(BF16)], axis=0))
            states[h] = states[h] * scale[c][h] + kv[c][h]
            o_h = o_h * jax.lax.rsqrt(jnp.mean(o_h * o_h, axis=-1, keepdims=True) + LN_EPS)
            heads.append(o_h)
        o_rows.append(jnp.concatenate(heads, axis=1))
    for h in range(GLA_HEADS):
        state_ref[h] = states[h]
    pgb = proj(OFF_GB, D_MODEL)
    y_b = (jnp.concatenate(o_rows, axis=0) * gn_ref[...]) * (r * _sigmoid(r))
    m = (m_acc + _sigmoid(pgb) * y_b).astype(BF16)

    rows = MIX_BLOCK // MIX_TAIL_PIECES
    for r0 in range(0, MIX_BLOCK, rows):
        h1 = alpha * x[r0:r0 + rows] + _dot(m[r0:r0 + rows], wout_ref[...])
        o_ref[0, r0:r0 + rows, :] = _layer_norm(h1, ln_g_ref[...], ln_b_ref[...])


def _ffn_kernel(h_ref, w1_ref, w2_ref, ln_g_ref, ln_b_ref, o_ref, *, alpha):
    h = h_ref[...]
    hb = h.astype(BF16)
    acc = alpha * h
    n_ff = D_FF // D_MODEL
    for j in range(n_ff):
        a = jnp.maximum(_dot(hb, w1_ref[:, j * D_MODEL:(j + 1) * D_MODEL]), 0.0)
        a = (a * a).astype(BF16)
        w2_j = w2_ref[j * D_MODEL:(j + 1) * D_MODEL, :]
        if j + 1 < n_ff:
            acc = acc + _dot(a, w2_j)
        else:
            rows = FFN_BLOCK // FFN_TAIL_PIECES
            for r0 in range(0, FFN_BLOCK, rows):
                out = acc[r0:r0 + rows] + _dot(a[r0:r0 + rows], w2_j)
                o_ref[r0:r0 + rows, :] = _layer_norm(out, ln_g_ref[...], ln_b_ref[...])


def _resident(shape):
    nd = len(shape)
    return pl.BlockSpec(shape, lambda *_: (0,) * nd, pipeline_mode=pl.Buffered(1))


def _mixer_call(x, w1, w2, consts, alpha):
    bsz, t, d = x.shape
    blocks_per_row = t // MIX_BLOCK
    n_steps = bsz * blocks_per_row
    slab1, slab2 = w1.shape[0] // n_steps, w2.shape[0] // n_steps

    def slab(b, i):
        return (b * blocks_per_row + i, 0)

    return pl.pallas_call(
        functools.partial(_mixer_kernel, alpha=alpha),
        out_shape=(jax.ShapeDtypeStruct((bsz, t, d), F32),
                   jax.ShapeDtypeStruct(w1.shape, BF16), jax.ShapeDtypeStruct(w2.shape, BF16)),
        grid=(bsz, blocks_per_row),
        in_specs=[pl.BlockSpec((1, MIX_BLOCK, d), lambda b, i: (b, i, 0)),
                  pl.BlockSpec((slab1, w1.shape[1]), slab), pl.BlockSpec((slab2, w2.shape[1]), slab)]
        + [pl.BlockSpec(memory_space=pl.ANY)] + [_resident(c.shape) for c in consts[1:]],
        out_specs=(pl.BlockSpec((1, MIX_BLOCK, d), lambda b, i: (b, i, 0)),
                   pl.BlockSpec((slab1, w1.shape[1]), slab), pl.BlockSpec((slab2, w2.shape[1]), slab)),
        scratch_shapes=[pltpu.VMEM((GLA_HEADS, GLA_HEAD_K, GLA_HEAD_V), F32),
                        pltpu.VMEM((d, D_MAIN), BF16),
                        pltpu.VMEM((d, d), BF16),
                        pltpu.VMEM((2, W_STAGE_ROWS, d), F32),
                        pltpu.SemaphoreType.DMA((2,))],
        compiler_params=pltpu.CompilerParams(
            dimension_semantics=("arbitrary", "arbitrary"), vmem_limit_bytes=VMEM_LIMIT),
        name="mixer",
    )(x, w1, w2, *consts)


def _ffn_call(h, w1, w2, ln_g, ln_b, alpha):
    n, d = h.shape
    return pl.pallas_call(
        functools.partial(_ffn_kernel, alpha=alpha),
        out_shape=jax.ShapeDtypeStruct((n, d), F32),
        grid=(n // FFN_BLOCK,),
        in_specs=[pl.BlockSpec((FFN_BLOCK, d), lambda i: (i, 0)),
                  _resident(w1.shape), _resident(w2.shape), _resident(ln_g.shape), _resident(ln_b.shape)],
        out_specs=pl.BlockSpec((FFN_BLOCK, d), lambda i: (i, 0)),
        compiler_params=pltpu.CompilerParams(
            dimension_semantics=("arbitrary",), vmem_limit_bytes=VMEM_LIMIT),
        name="ffn",
    )(h, w1, w2, ln_g, ln_b)


def kernel(x, w_in, b_in, sg_ln_g, sg_ln_b, sg_w_s, sg_b_s, gla_w_gate2, gla_b_gate, gla_norm_g,
           w_out, ln1_g, ln1_b, w_ff1, w_ff2, ln2_g, ln2_b):
    bsz, t, d = x.shape
    depth = w_in.shape[0]
    alpha = (2.0 * depth) ** 0.25
    table_np, level_np = _decay_tables()
    table = jnp.asarray(np.concatenate([table_np, table_np], axis=1), BF16)
    level = jnp.asarray(np.concatenate([level_np, level_np], axis=1))
    upper = jnp.asarray(np.stack([np.broadcast_to(((np.arange(GLA_CHUNK) >> j) & 1)[:, None],
                                                  (GLA_CHUNK, GLA_KEY_DIM)) for j in range(3)]).astype(np.int32))
    low0, low1 = OFF_LOW, OFF_LOW + GLA_GATE_RANK
    rank = GLA_GATE_RANK
    h = x
    for l in range(depth):
        w_main = jnp.transpose(w_in[l])
        b_main = jnp.concatenate([b_in[l][:low0], b_in[l][low1:]])[None, :]
        w_low = jnp.pad(jnp.tile(jnp.transpose(w_main[low0:low1]).astype(BF16), (1, 3)),
                        ((0, 0), (0, 2 * LANES - 3 * rank)))
        b_low = jnp.pad(jnp.tile(b_in[l][low0:low1], 3), (0, 2 * LANES - 3 * rank))[None, :]
        wg_hi = gla_w_gate2[l].astype(BF16)
        wg_lo = (gla_w_gate2[l] - wg_hi.astype(F32)).astype(BF16)
        wg = jnp.pad(jnp.concatenate([wg_hi, wg_hi, wg_lo], axis=0), ((0, LANES - 3 * rank), (0, 0)))
        bs_plane = jnp.repeat(jnp.transpose(sg_b_s[l]), D_MODEL // SG_GROUPS, axis=1)
        consts = (w_main, b_main, w_low, b_low, wg, gla_b_gate[l][None, :],
                  sg_ln_g[l][None, :], sg_ln_b[l][None, :], sg_w_s[l], bs_plane, table, level, upper,
                  gla_norm_g[l][None, :], w_out[l], ln1_g[l][None, :], ln1_b[l][None, :])
        h, w1b, w2b = _mixer_call(h, w_ff1[l], w_ff2[l], consts, alpha)
        h = _ffn_call(h.reshape(bsz * t, d), w1b, w2b,
                      ln2_g[l][None, :], ln2_b[l][None, :], alpha).reshape(bsz, t, d)
    return h
```

```python
import functools
import math

import jax
import jax.numpy as jnp
import numpy as np
from jax.experimental import pallas as pl
from jax.experimental.pallas import tpu as pltpu

D_MODEL = 1024
SG_CHUNK = 128
SG_GROUPS = 8
GLA_HEADS = 4
GLA_KEY_DIM = 512
GLA_VAL_DIM = 1024
GLA_HEAD_K = GLA_KEY_DIM // GLA_HEADS
GLA_HEAD_V = GLA_VAL_DIM // GLA_HEADS
GLA_GATE_RANK = 16
GLA_GATE_TEMP = 16.0
D_FF = 4 * D_MODEL
LN_EPS = 1e-5

LANES = 128
BF16_ROWS = 16
GLA_CHUNK = 128
GLA_LEVELS = int(math.log2(GLA_CHUNK))
MIX_BLOCK = 512
FFN_BLOCK = 1024
MIX_TAIL_ROWS = 256
FFN_TAIL_PIECES = 4
W_STAGE_ROWS = 512
VMEM_LIMIT = 56 * 1024 * 1024

OFF_U, OFF_V, OFF_Q, OFF_K, OFF_GV, OFF_R, OFF_LOW, OFF_GA, OFF_GB = 0, 1024, 2048, 2560, 3072, 4096, 5120, 5136, 6160
D_MAIN = 7168

GELU_C = 0.7978845608028654
LOG2E = 1.4426950408889634

BF16 = jnp.bfloat16
F32 = jnp.float32


def _decay_tables():
    c = GLA_CHUNK
    blocks = []
    for lw in range(GLA_LEVELS - 1, -1, -1):
        a = np.zeros((c, c), np.float32)
        for t in range(c):
            if (t >> lw) & 1:
                a[t, (t >> lw) << lw:t + 1] = 1.0
            else:
                a[t, t + 1:((t >> lw) + 1) << lw] = 1.0
        blocks.append(a)
    blocks.append(np.tril(np.ones((c, c), np.float32)))
    table = np.concatenate(blocks, axis=0)
    t = np.arange(c)[:, None]
    s = np.arange(c)[None, :]
    x = t ^ s
    top = np.zeros_like(x)
    nz = x > 0
    top[nz] = np.floor(np.log2(x[nz])).astype(x.dtype)
    level = np.where(t > s, GLA_LEVELS - 1 - top, np.where(t == s, GLA_LEVELS, -1)).astype(np.int32)
    return table, level


def _layer_norm(x, g, b):
    mu = jnp.mean(x, axis=-1, keepdims=True)
    xc = x - mu
    var = jnp.mean(xc * xc, axis=-1, keepdims=True)
    return xc * jax.lax.rsqrt(var + LN_EPS) * g + b


def _gelu(x):
    hx = 0.5 * x
    return hx + hx * jnp.tanh(x * (GELU_C + (GELU_C * 0.044715) * (x * x)))


def _sigmoid(x):
    return 0.5 * jnp.tanh(0.5 * x) + 0.5


def _split_bf16(x):
    hi = x.astype(BF16)
    lo = (x - hi.astype(F32)).astype(BF16)
    return hi, lo


def _dot(a, b):
    return jnp.dot(a, b, preferred_element_type=F32)


def _dot_tn(a, b):
    return jax.lax.dot_general(a, b, (((0,), (0,)), ((), ())), preferred_element_type=F32)


def _cast_projection_weights(w_hbm, w_ref, stage_ref, sem):
    starts = [r for r in range(0, D_MAIN + GLA_GATE_RANK, W_STAGE_ROWS) if r < OFF_LOW]
    starts += list(range(OFF_LOW + GLA_GATE_RANK, D_MAIN + GLA_GATE_RANK, W_STAGE_ROWS))

    def copy(j):
        return pltpu.make_async_copy(w_hbm.at[pl.ds(starts[j], W_STAGE_ROWS)], stage_ref.at[j % 2], sem.at[j % 2])

    copy(0).start()
    for j, r0 in enumerate(starts):
        if j + 1 < len(starts):
            copy(j + 1).start()
        copy(j).wait()
        c0 = r0 if r0 < OFF_LOW else r0 - GLA_GATE_RANK
        w_ref[:, c0:c0 + W_STAGE_ROWS] = jnp.transpose(stage_ref[j % 2]).astype(BF16)


def _mixer_kernel(x_ref, w1f_ref, w2f_ref, w_hbm, b_ref, wlow_ref, blow_ref, wg_ref, bg_ref,
                  sg_g_ref, sg_b_ref, ws_ref, bs_ref, tab_ref, lvl_ref, upper_ref, gn_ref,
                  woutf_ref, ln_g_ref, ln_b_ref, o_ref, w1b_ref, w2b_ref,
                  state_ref, w_ref, wout_ref, stage_ref, sem, *, alpha):
    @pl.when((pl.program_id(0) == 0) & (pl.program_id(1) == 0))
    def _():
        _cast_projection_weights(w_hbm, w_ref, stage_ref, sem)
        wout_ref[...] = woutf_ref[...].astype(BF16)

    @pl.when(pl.program_id(1) == 0)
    def _():
        state_ref[...] = jnp.zeros_like(state_ref)

    w1b_ref[...] = w1f_ref[...].astype(BF16)
    w2b_ref[...] = w2f_ref[...].astype(BF16)

    x = x_ref[0]
    xh = x.astype(BF16)
    n_chunks = MIX_BLOCK // GLA_CHUNK

    def proj(off, width):
        c0 = off if off < OFF_LOW else off - GLA_GATE_RANK
        return _dot(xh, w_ref[:, c0:c0 + width]) + b_ref[:, c0:c0 + width]

    pv = proj(OFF_V, D_MODEL)
    q = proj(OFF_Q, GLA_KEY_DIM) * (GLA_HEAD_K ** -0.5)
    k = proj(OFF_K, GLA_KEY_DIM)
    a3 = (_dot(xh, wlow_ref[...]) + blow_ref[...])[:, :LANES]

    v_sg = _layer_norm(_gelu(pv), sg_g_ref[...], sg_b_ref[...]).astype(BF16)
    a3_hi = a3.astype(BF16).astype(F32)
    lane = jax.lax.broadcasted_iota(jnp.int32, a3.shape, 1)
    mid = (lane >= GLA_GATE_RANK) & (lane < 2 * GLA_GATE_RANK)
    z = _dot(jnp.where(mid, a3 - a3_hi, a3_hi).astype(BF16), wg_ref[...]) + bg_ref[...]
    pu = proj(OFF_U, D_MODEL)
    log_f = (jnp.minimum(z, 0.0) - jnp.log(1.0 + jnp.exp(-jnp.abs(z)))) * (LOG2E / GLA_GATE_TEMP)

    row = jax.lax.broadcasted_iota(jnp.int32, (SG_CHUNK, SG_CHUNK), 0)
    col = jax.lax.broadcasted_iota(jnp.int32, (SG_CHUNK, SG_CHUNK), 1)
    mixed_g = []
    for g in range(SG_GROUPS):
        w_g = jnp.where(row >= col, ws_ref[g], 0.0).astype(BF16)
        gs = slice(g * LANES, (g + 1) * LANES)
        rhs = jnp.concatenate([v_sg[c * SG_CHUNK:(c + 1) * SG_CHUNK, gs] for c in range(MIX_BLOCK // SG_CHUNK)],
                              axis=1)
        mixed_g.append(_dot(w_g, rhs))
    mixed = jnp.concatenate(
        [jnp.concatenate([mg[:, c * LANES:(c + 1) * LANES] for mg in mixed_g], axis=1) + bs_ref[...]
         for c in range(MIX_BLOCK // SG_CHUNK)], axis=0)
    pga = proj(OFF_GA, D_MODEL)

    decays, prefixes, suffixes = [], [], []
    for c in range(n_chunks):
        f_hi, f_lo = _split_bf16(log_f[c * GLA_CHUNK:(c + 1) * GLA_CHUNK])
        expo = _dot(tab_ref[...], jnp.concatenate([f_hi, f_lo], axis=0))
        cum = expo[GLA_LEVELS * GLA_CHUNK:]
        decays.append(jnp.exp2(expo[:GLA_LEVELS * GLA_CHUNK]).astype(BF16))
        prefixes.append(jnp.exp2(cum))
        suffixes.append(jnp.exp2(cum[GLA_CHUNK - 1:GLA_CHUNK] - cum))

    lvl2 = lvl_ref[...]
    pair_w = 2 * GLA_HEAD_K
    no_keys = jnp.zeros((GLA_HEAD_K, GLA_CHUNK), BF16)

    def pair_scores(lhs, keys_t):
        out = []
        for p in range(GLA_HEADS // 2):
            even = keys_t[(2 * p) * GLA_HEAD_K:(2 * p + 1) * GLA_HEAD_K]
            odd = keys_t[(2 * p + 1) * GLA_HEAD_K:(2 * p + 2) * GLA_HEAD_K]
            bd = jnp.concatenate([jnp.concatenate([even, no_keys], axis=1),
                                  jnp.concatenate([no_keys, odd], axis=1)], axis=0)
            out.append(_dot(lhs[:, p * pair_w:(p + 1) * pair_w], bd))
        return out

    def level_update(scores, c, lev):
        qc = q[c * GLA_CHUNK:(c + 1) * GLA_CHUNK]
        kc = k[c * GLA_CHUNK:(c + 1) * GLA_CHUNK]
        w = GLA_CHUNK >> (lev + 1)
        n_blk = GLA_CHUNK // w
        if w >= 8:
            mixed_qk = jnp.concatenate([(qc if (b % 2) else kc)[b * w:(b + 1) * w] for b in range(n_blk)], axis=0)
        else:
            mixed_qk = jnp.where(upper_ref[GLA_LEVELS - 1 - lev] != 0, qc, kc)
        zl = mixed_qk.astype(BF16) * decays[c][lev * GLA_CHUNK:(lev + 1) * GLA_CHUNK]
        zl_t = jnp.transpose(zl)
        if w >= BF16_ROWS:
            lhs = jnp.concatenate([zl[b * w:(b + 1) * w] for b in range(1, n_blk, 2)], axis=0)
            parts = pair_scores(lhs, zl_t)
            out = []
            for p in range(GLA_HEADS // 2):
                rows = []
                for b in range(n_blk):
                    old = scores[p][b * w:(b + 1) * w]
                    if b % 2:
                        new = parts[p][(b // 2) * w:(b // 2 + 1) * w]
                        old = jnp.where(lvl2[b * w:(b + 1) * w] == lev, new, old)
                    rows.append(old)
                out.append(jnp.concatenate(rows, axis=0))
            return out
        parts = pair_scores(zl, zl_t)
        return [jnp.where(lvl2 == lev, parts[p], scores[p]) for p in range(GLA_HEADS // 2)]

    def diagonal_update(scores, c):
        qk = q[c * GLA_CHUNK:(c + 1) * GLA_CHUNK] * k[c * GLA_CHUNK:(c + 1) * GLA_CHUNK]
        out = []
        for p in range(GLA_HEADS // 2):
            diag = [jnp.broadcast_to(jnp.sum(qk[:, h * GLA_HEAD_K:(h + 1) * GLA_HEAD_K], axis=1, keepdims=True),
                                     (GLA_CHUNK, GLA_CHUNK)) for h in (2 * p, 2 * p + 1)]
            out.append(jnp.where(lvl2 == GLA_LEVELS, jnp.concatenate(diag, axis=1), scores[p]))
        return out

    all_scores = [[jnp.zeros((GLA_CHUNK, 2 * GLA_CHUNK), F32) for _ in range(GLA_HEADS // 2)]
                  for _ in range(n_chunks)]
    m_acc = _sigmoid(pga) * (_gelu(pu) * mixed)
    for lev in range(GLA_LEVELS):
        all_scores = [level_update(all_scores[c], c, lev) for c in range(n_chunks)]
        if lev == 1:
            gv = proj(OFF_GV, GLA_VAL_DIM).astype(BF16)
        if lev == 4:
            r = proj(OFF_R, GLA_VAL_DIM)
    all_scores = [diagonal_update(all_scores[c], c) for c in range(n_chunks)]

    heads_sl = [slice(h * GLA_HEAD_K, (h + 1) * GLA_HEAD_K) for h in range(GLA_HEADS)]
    v_heads = [[gv[c * GLA_CHUNK:(c + 1) * GLA_CHUNK, h * GLA_HEAD_V:(h + 1) * GLA_HEAD_V]
                for h in range(GLA_HEADS)] for c in range(n_chunks)]
    kv, scale, q_ins = [], [], []
    for c in range(n_chunks):
        r0 = c * GLA_CHUNK
        q_ins.append((q[r0:r0 + GLA_CHUNK] * prefixes[c]).astype(BF16))
        k_out = (k[r0:r0 + GLA_CHUNK] * suffixes[c]).astype(BF16)
        total = prefixes[c][GLA_CHUNK - 1:GLA_CHUNK]
        kv.append([_dot_tn(k_out[:, sl], v_heads[c][h]) for h, sl in enumerate(heads_sl)])
        tcols = [jnp.transpose(jnp.broadcast_to(total[:, sl], (GLA_HEAD_K, GLA_HEAD_K))) for sl in heads_sl]
        scale.append([jnp.concatenate([t, t], axis=1) for t in tcols])
    states = [state_ref[h] for h in range(GLA_HEADS)]
    o_rows = []
    for c in range(n_chunks):
        heads = []
        for h, sl in enumerate(heads_sl):
            s_h = all_scores[c][h // 2][:, (h % 2) * GLA_CHUNK:(h % 2 + 1) * GLA_CHUNK].astype(BF16)
            o_h = _dot(jnp.concatenate([s_h, q_ins[c][:, sl]], axis=1),
                       jnp.concatenate([v_heads[c][h], states[h].astype(BF16)], axis=0))
            states[h] = states[h] * scale[c][h] + kv[c][h]
            o_h = o_h * jax.lax.rsqrt(jnp.mean(o_h * o_h, axis=-1, keepdims=True) + LN_EPS)
            heads.append(o_h)
        o_rows.append(jnp.concatenate(heads, axis=1))
    for h in range(GLA_HEADS):
        state_ref[h] = states[h]
    pgb = proj(OFF_GB, D_MODEL)
    y_b = (jnp.concatenate(o_rows, axis=0) * gn_ref[...]) * (r * _sigmoid(r))
    m = (m_acc + _sigmoid(pgb) * y_b).astype(BF16)

    for r0, r1 in ((0, MIX_BLOCK - MIX_TAIL_ROWS), (MIX_BLOCK - MIX_TAIL_ROWS, MIX_BLOCK)):
        h1 = alpha * x_ref[0, r0:r1, :] + _dot(m[r0:r1], wout_ref[...])
        o_ref[0, r0:r1, :] = _layer_norm(h1, ln_g_ref[...], ln_b_ref[...])


def _ffn_kernel(h_ref, w1_ref, w2_ref, ln_g_ref, ln_b_ref, o_ref, *, alpha):
    h = h_ref[...]
    hb = h.astype(BF16)
    acc = alpha * h
    n_ff = D_FF // D_MODEL
    for j in range(n_ff):
        a = jnp.maximum(_dot(hb, w1_ref[:, j * D_MODEL:(j + 1) * D_MODEL]), 0.0)
        a = (a * a).astype(BF16)
        w2_j = w2_ref[j * D_MODEL:(j + 1) * D_MODEL, :]
        if j + 1 < n_ff:
            acc = acc + _dot(a, w2_j)
        else:
            rows = FFN_BLOCK // FFN_TAIL_PIECES
            for r0 in range(0, FFN_BLOCK, rows):
                out = acc[r0:r0 + rows] + _dot(a[r0:r0 + rows], w2_j)
                o_ref[r0:r0 + rows, :] = _layer_norm(out, ln_g_ref[...], ln_b_ref[...])


def _resident(shape):
    nd = len(shape)
    return pl.BlockSpec(shape, lambda *_: (0,) * nd, pipeline_mode=pl.Buffered(1))


def _mixer_call(x, w1, w2, consts, alpha):
    bsz, t, d = x.shape
    blocks_per_row = t // MIX_BLOCK
    n_steps = bsz * blocks_per_row
    slab1, slab2 = w1.shape[0] // n_steps, w2.shape[0] // n_steps

    def slab(b, i):
        return (b * blocks_per_row + i, 0)

    return pl.pallas_call(
        functools.partial(_mixer_kernel, alpha=alpha),
        out_shape=(jax.ShapeDtypeStruct((bsz, t, d), F32),
                   jax.ShapeDtypeStruct(w1.shape, BF16), jax.ShapeDtypeStruct(w2.shape, BF16)),
        grid=(bsz, blocks_per_row),
        in_specs=[pl.BlockSpec((1, MIX_BLOCK, d), lambda b, i: (b, i, 0)),
                  pl.BlockSpec((slab1, w1.shape[1]), slab), pl.BlockSpec((slab2, w2.shape[1]), slab)]
        + [pl.BlockSpec(memory_space=pl.ANY)] + [_resident(c.shape) for c in consts[1:]],
        out_specs=(pl.BlockSpec((1, MIX_BLOCK, d), lambda b, i: (b, i, 0)),
                   pl.BlockSpec((slab1, w1.shape[1]), slab), pl.BlockSpec((slab2, w2.shape[1]), slab)),
        scratch_shapes=[pltpu.VMEM((GLA_HEADS, GLA_HEAD_K, GLA_HEAD_V), F32),
                        pltpu.VMEM((d, D_MAIN), BF16),
                        pltpu.VMEM((d, d), BF16),
                        pltpu.VMEM((2, W_STAGE_ROWS, d), F32),
                        pltpu.SemaphoreType.DMA((2,))],
        compiler_params=pltpu.CompilerParams(
            dimension_semantics=("arbitrary", "arbitrary"), vmem_limit_bytes=VMEM_LIMIT),
        name="mixer",
    )(x, w1, w2, *consts)


def _ffn_call(h, w1, w2, ln_g, ln_b, alpha):
    n, d = h.shape
    return pl.pallas_call(
        functools.partial(_ffn_kernel, alpha=alpha),
        out_shape=jax.ShapeDtypeStruct((n, d), F32),
        grid=(n // FFN_BLOCK,),
        in_specs=[pl.BlockSpec((FFN_BLOCK, d), lambda i: (i, 0)),
                  _resident(w1.shape), _resident(w2.shape), _resident(ln_g.shape), _resident(ln_b.shape)],
        out_specs=pl.BlockSpec((FFN_BLOCK, d), lambda i: (i, 0)),
        compiler_params=pltpu.CompilerParams(
            dimension_semantics=("arbitrary",), vmem_limit_bytes=VMEM_LIMIT),
        name="ffn",
    )(h, w1, w2, ln_g, ln_b)


def kernel(x, w_in, b_in, sg_ln_g, sg_ln_b, sg_w_s, sg_b_s, gla_w_gate2, gla_b_gate, gla_norm_g,
           w_out, ln1_g, ln1_b, w_ff1, w_ff2, ln2_g, ln2_b):
    bsz, t, d = x.shape
    depth = w_in.shape[0]
    alpha = (2.0 * depth) ** 0.25
    table_np, level_np = _decay_tables()
    table = jnp.asarray(np.concatenate([table_np, table_np], axis=1), BF16)
    level = jnp.asarray(np.concatenate([level_np, level_np], axis=1))
    upper = jnp.asarray(np.stack([np.broadcast_to(((np.arange(GLA_CHUNK) >> j) & 1)[:, None],
                                                  (GLA_CHUNK, GLA_KEY_DIM)) for j in range(3)]).astype(np.int32))
    low0, low1 = OFF_LOW, OFF_LOW + GLA_GATE_RANK
    rank = GLA_GATE_RANK
    h = x
    for l in range(depth):
        w_main = jnp.transpose(w_in[l])
        b_main = jnp.concatenate([b_in[l][:low0], b_in[l][low1:]])[None, :]
        w_low = jnp.pad(jnp.tile(jnp.transpose(w_main[low0:low1]).astype(BF16), (1, 3)),
                        ((0, 0), (0, 2 * LANES - 3 * rank)))
        b_low = jnp.pad(jnp.tile(b_in[l][low0:low1], 3), (0, 2 * LANES - 3 * rank))[None, :]
        wg_hi = gla_w_gate2[l].astype(BF16)
        wg_lo = (gla_w_gate2[l] - wg_hi.astype(F32)).astype(BF16)
        wg = jnp.pad(jnp.concatenate([wg_hi, wg_hi, wg_lo], axis=0), ((0, LANES - 3 * rank), (0, 0)))
        bs_plane = jnp.repeat(jnp.transpose(sg_b_s[l]), D_MODEL // SG_GROUPS, axis=1)
        consts = (w_main, b_main, w_low, b_low, wg, gla_b_gate[l][None, :],
                  sg_ln_g[l][None, :], sg_ln_b[l][None, :], sg_w_s[l], bs_plane, table, level, upper,
                  gla_norm_g[l][None, :], w_out[l], ln1_g[l][None, :], ln1_b[l][None, :])
        h, w1b, w2b = _mixer_call(h, w_ff1[l], w_ff2[l], consts, alpha)
        h = _ffn_call(h.reshape(bsz * t, d), w1b, w2b,
                      ln2_g[l][None, :], ln2_b[l][None, :], alpha).reshape(bsz, t, d)
    return h
```

```python
import functools
import math

import jax
import jax.numpy as jnp
import numpy as np
from jax.experimental import pallas as pl
from jax.experimental.pallas import tpu as pltpu

D_MODEL = 1024
SG_CHUNK = 128
SG_GROUPS = 8
GLA_HEADS = 4
GLA_KEY_DIM = 512
GLA_VAL_DIM = 1024
GLA_HEAD_K = GLA_KEY_DIM // GLA_HEADS
GLA_HEAD_V = GLA_VAL_DIM // GLA_HEADS
GLA_GATE_RANK = 16
GLA_GATE_TEMP = 16.0
D_FF = 4 * D_MODEL
LN_EPS = 1e-5

LANES = 128
BF16_ROWS = 16
GLA_CHUNK = 128
GLA_LEVELS = int(math.log2(GLA_CHUNK))
MIX_BLOCK = 512
FFN_BLOCK = 1024
MIX_TAIL_ROWS = 256
FFN_TAIL_PIECES = 4
W_STAGE_ROWS = 512
VMEM_LIMIT = 58 * 1024 * 1024
SINGLE_BUFFER_BYTES = 2 * 1024 * 1024

OFF_U, OFF_V, OFF_Q, OFF_K, OFF_GV, OFF_R, OFF_LOW, OFF_GA, OFF_GB = 0, 1024, 2048, 2560, 3072, 4096, 5120, 5136, 6160
D_MAIN = 7168

GELU_C = 0.7978845608028654
LOG2E = 1.4426950408889634

BF16 = jnp.bfloat16
F32 = jnp.float32


def _decay_tables():
    c = GLA_CHUNK
    blocks = []
    for lw in range(GLA_LEVELS - 1, -1, -1):
        a = np.zeros((c, c), np.float32)
        for t in range(c):
            if (t >> lw) & 1:
                a[t, (t >> lw) << lw:t + 1] = 1.0
            else:
                a[t, t + 1:((t >> lw) + 1) << lw] = 1.0
        blocks.append(a)
    blocks.append(np.tril(np.ones((c, c), np.float32)))
    table = np.concatenate(blocks, axis=0)
    t = np.arange(c)[:, None]
    s = np.arange(c)[None, :]
    x = t ^ s
    top = np.zeros_like(x)
    nz = x > 0
    top[nz] = np.floor(np.log2(x[nz])).astype(x.dtype)
    level = np.where(t > s, GLA_LEVELS - 1 - top, np.where(t == s, GLA_LEVELS, -1)).astype(np.int32)
    return table, level


def _layer_norm(x, g, b):
    mu = jnp.mean(x, axis=-1, keepdims=True)
    xc = x - mu
    var = jnp.mean(xc * xc, axis=-1, keepdims=True)
    return xc * jax.lax.rsqrt(var + LN_EPS) * g + b


def _gelu(x):
    hx = 0.5 * x
    return hx + hx * jnp.tanh(x * (GELU_C + (GELU_C * 0.044715) * (x * x)))


def _sigmoid(x):
    return 0.5 * jnp.tanh(0.5 * x) + 0.5


def _split_bf16(x):
    hi = x.astype(BF16)
    lo = (x - hi.astype(F32)).astype(BF16)
    return hi, lo


def _dot(a, b):
    return jnp.dot(a, b, preferred_element_type=F32)


def _dot_tn(a, b):
    return jax.lax.dot_general(a, b, (((0,), (0,)), ((), ())), preferred_element_type=F32)


def _cast_projection_weights(w_hbm, w_ref, stage_ref, sem):
    starts = [r for r in range(0, D_MAIN + GLA_GATE_RANK, W_STAGE_ROWS) if r < OFF_LOW]
    starts += list(range(OFF_LOW + GLA_GATE_RANK, D_MAIN + GLA_GATE_RANK, W_STAGE_ROWS))

    def copy(j):
        return pltpu.make_async_copy(w_hbm.at[pl.ds(starts[j], W_STAGE_ROWS)], stage_ref.at[j % 2], sem.at[j % 2])

    copy(0).start()
    for j, r0 in enumerate(starts):
        if j + 1 < len(starts):
            copy(j + 1).start()
        copy(j).wait()
        c0 = r0 if r0 < OFF_LOW else r0 - GLA_GATE_RANK
        w_ref[:, c0:c0 + W_STAGE_ROWS] = jnp.transpose(stage_ref[j % 2]).astype(BF16)


def _mixer_kernel(x_ref, w1f_ref, w2f_ref, w_hbm, b_ref, wlow_ref, blow_ref, wg_ref, bg_ref,
                  sg_g_ref, sg_b_ref, ws_ref, bs_ref, tab_ref, lvl_ref, upper_ref, gn_ref,
                  woutf_ref, ln_g_ref, ln_b_ref, o_ref, w1b_ref, w2b_ref,
                  state_ref, w_ref, wout_ref, stage_ref, sem, *, alpha):
    @pl.when((pl.program_id(0) == 0) & (pl.program_id(1) == 0))
    def _():
        _cast_projection_weights(w_hbm, w_ref, stage_ref, sem)
        wout_ref[...] = woutf_ref[...].astype(BF16)

    @pl.when(pl.program_id(1) == 0)
    def _():
        state_ref[...] = jnp.zeros_like(state_ref)

    w1b_ref[...] = w1f_ref[...].astype(BF16)
    w2b_ref[...] = w2f_ref[...].astype(BF16)

    x = x_ref[0]
    xh = x.astype(BF16)
    n_chunks = MIX_BLOCK // GLA_CHUNK

    def proj(off, width):
        c0 = off if off < OFF_LOW else off - GLA_GATE_RANK
        return _dot(xh, w_ref[:, c0:c0 + width]) + b_ref[:, c0:c0 + width]

    pv = proj(OFF_V, D_MODEL)
    q = proj(OFF_Q, GLA_KEY_DIM) * (GLA_HEAD_K ** -0.5)
    k = proj(OFF_K, GLA_KEY_DIM)
    a3 = (_dot(xh, wlow_ref[...]) + blow_ref[...])[:, :LANES]

    v_sg = _layer_norm(_gelu(pv), sg_g_ref[...], sg_b_ref[...]).astype(BF16)
    a3_hi = a3.astype(BF16).astype(F32)
    lane = jax.lax.broadcasted_iota(jnp.int32, a3.shape, 1)
    mid = (lane >= GLA_GATE_RANK) & (lane < 2 * GLA_GATE_RANK)
    z = _dot(jnp.where(mid, a3 - a3_hi, a3_hi).astype(BF16), wg_ref[...]) + bg_ref[...]
    pu = proj(OFF_U, D_MODEL)
    log_f = (jnp.minimum(z, 0.0) - jnp.log(1.0 + jnp.exp(-jnp.abs(z)))) * (LOG2E / GLA_GATE_TEMP)

    row = jax.lax.broadcasted_iota(jnp.int32, (SG_CHUNK, SG_CHUNK), 0)
    col = jax.lax.broadcasted_iota(jnp.int32, (SG_CHUNK, SG_CHUNK), 1)
    mixed_g = []
    for g in range(SG_GROUPS):
        w_g = jnp.where(row >= col, ws_ref[g], 0.0).astype(BF16)
        gs = slice(g * LANES, (g + 1) * LANES)
        rhs = jnp.concatenate([v_sg[c * SG_CHUNK:(c + 1) * SG_CHUNK, gs] for c in range(MIX_BLOCK // SG_CHUNK)],
                              axis=1)
        mixed_g.append(_dot(w_g, rhs))
    mixed = jnp.concatenate(
        [jnp.concatenate([mg[:, c * LANES:(c + 1) * LANES] for mg in mixed_g], axis=1) + bs_ref[...]
         for c in range(MIX_BLOCK // SG_CHUNK)], axis=0)
    pga = proj(OFF_GA, D_MODEL)

    decays, prefixes, suffixes = [], [], []
    for c in range(n_chunks):
        f_hi, f_lo = _split_bf16(log_f[c * GLA_CHUNK:(c + 1) * GLA_CHUNK])
        expo = _dot(tab_ref[...], jnp.concatenate([f_hi, f_lo], axis=0))
        cum = expo[GLA_LEVELS * GLA_CHUNK:]
        decays.append(jnp.exp2(expo[:GLA_LEVELS * GLA_CHUNK]).astype(BF16))
        prefixes.append(jnp.exp2(cum))
        suffixes.append(jnp.exp2(cum[GLA_CHUNK - 1:GLA_CHUNK] - cum))

    lvl2 = lvl_ref[...]
    pair_w = 2 * GLA_HEAD_K
    no_keys = jnp.zeros((GLA_HEAD_K, GLA_CHUNK), BF16)

    def pair_scores(lhs, keys_t):
        out = []
        for p in range(GLA_HEADS // 2):
            even = keys_t[(2 * p) * GLA_HEAD_K:(2 * p + 1) * GLA_HEAD_K]
            odd = keys_t[(2 * p + 1) * GLA_HEAD_K:(2 * p + 2) * GLA_HEAD_K]
            bd = jnp.concatenate([jnp.concatenate([even, no_keys], axis=1),
                                  jnp.concatenate([no_keys, odd], axis=1)], axis=0)
            out.append(_dot(lhs[:, p * pair_w:(p + 1) * pair_w], bd))
        return out

    def level_update(scores, c, lev):
        qc = q[c * GLA_CHUNK:(c + 1) * GLA_CHUNK]
        kc = k[c * GLA_CHUNK:(c + 1) * GLA_CHUNK]
        w = GLA_CHUNK >> (lev + 1)
        n_blk = GLA_CHUNK // w
        if w >= 8:
            mixed_qk = jnp.concatenate([(qc if (b % 2) else kc)[b * w:(b + 1) * w] for b in range(n_blk)], axis=0)
        else:
            mixed_qk = jnp.where(upper_ref[GLA_LEVELS - 1 - lev] != 0, qc, kc)
        zl = mixed_qk.astype(BF16) * decays[c][lev * GLA_CHUNK:(lev + 1) * GLA_CHUNK]
        zl_t = jnp.transpose(zl)
        if w >= BF16_ROWS:
            lhs = jnp.concatenate([zl[b * w:(b + 1) * w] for b in range(1, n_blk, 2)], axis=0)
            parts = pair_scores(lhs, zl_t)
            out = []
            for p in range(GLA_HEADS // 2):
                rows = []
                for b in range(n_blk):
                    old = scores[p][b * w:(b + 1) * w]
                    if b % 2:
                        new = parts[p][(b // 2) * w:(b // 2 + 1) * w]
                        old = jnp.where(lvl2[b * w:(b + 1) * w] == lev, new, old)
                    rows.append(old)
                out.append(jnp.concatenate(rows, axis=0))
            return out
        parts = pair_scores(zl, zl_t)
        return [jnp.where(lvl2 == lev, parts[p], scores[p]) for p in range(GLA_HEADS // 2)]

    def diagonal_update(scores, c):
        qk = q[c * GLA_CHUNK:(c + 1) * GLA_CHUNK] * k[c * GLA_CHUNK:(c + 1) * GLA_CHUNK]
        out = []
        for p in range(GLA_HEADS // 2):
            diag = [jnp.broadcast_to(jnp.sum(qk[:, h * GLA_HEAD_K:(h + 1) * GLA_HEAD_K], axis=1, keepdims=True),
                                     (GLA_CHUNK, GLA_CHUNK)) for h in (2 * p, 2 * p + 1)]
            out.append(jnp.where(lvl2 == GLA_LEVELS, jnp.concatenate(diag, axis=1), scores[p]))
        return out

    all_scores = [[jnp.zeros((GLA_CHUNK, 2 * GLA_CHUNK), F32) for _ in range(GLA_HEADS // 2)]
                  for _ in range(n_chunks)]
    m_acc = _sigmoid(pga) * (_gelu(pu) * mixed)
    for lev in range(GLA_LEVELS):
        all_scores = [level_update(all_scores[c], c, lev) for c in range(n_chunks)]
        if lev == 1:
            gv = proj(OFF_GV, GLA_VAL_DIM).astype(BF16)
        if lev == 4:
            r = proj(OFF_R, GLA_VAL_DIM)
    all_scores = [diagonal_update(all_scores[c], c) for c in range(n_chunks)]

    heads_sl = [slice(h * GLA_HEAD_K, (h + 1) * GLA_HEAD_K) for h in range(GLA_HEADS)]
    v_heads = [[gv[c * GLA_CHUNK:(c + 1) * GLA_CHUNK, h * GLA_HEAD_V:(h + 1) * GLA_HEAD_V]
                for h in range(GLA_HEADS)] for c in range(n_chunks)]
    kv, scale, q_ins = [], [], []
    for c in range(n_chunks):
        r0 = c * GLA_CHUNK
        q_ins.append((q[r0:r0 + GLA_CHUNK] * prefixes[c]).astype(BF16))
        k_out = (k[r0:r0 + GLA_CHUNK] * suffixes[c]).astype(BF16)
        total = prefixes[c][GLA_CHUNK - 1:GLA_CHUNK]
        kv.append([_dot_tn(k_out[:, sl], v_heads[c][h]) for h, sl in enumerate(heads_sl)])
        tcols = [jnp.transpose(jnp.broadcast_to(total[:, sl], (GLA_HEAD_K, GLA_HEAD_K))) for sl in heads_sl]
        scale.append([jnp.concatenate([t, t], axis=1) for t in tcols])
    states = [state_ref[h] for h in range(GLA_HEADS)]
    o_rows = []
    for c in range(n_chunks):
        heads = []
        for h, sl in enumerate(heads_sl):
            s_h = all_scores[c][h // 2][:, (h % 2) * GLA_CHUNK:(h % 2 + 1) * GLA_CHUNK].astype(BF16)
            o_h = _dot(jnp.concatenate([s_h, q_ins[c][:, sl]], axis=1),
                       jnp.concatenate([v_heads[c][h], states[h].astype(BF16)], axis=0))
            states[h] = states[h] * scale[c][h] + kv[c][h]
            o_h = o_h * jax.lax.rsqrt(jnp.mean(o_h * o_h, axis=-1, keepdims=True) + LN_EPS)
            heads.append(o_h)
        o_rows.append(jnp.concatenate(heads, axis=1))
    for h in range(GLA_HEADS):
        state_ref[h] = states[h]
    pgb = proj(OFF_GB, D_MODEL)
    y_b = (jnp.concatenate(o_rows, axis=0) * gn_ref[...]) * (r * _sigmoid(r))
    m = (m_acc + _sigmoid(pgb) * y_b).astype(BF16)

    for r0, r1 in ((0, MIX_BLOCK - MIX_TAIL_ROWS), (MIX_BLOCK - MIX_TAIL_ROWS, MIX_BLOCK)):
        h1 = alpha * x_ref[0, r0:r1, :] + _dot(m[r0:r1], wout_ref[...])
        o_ref[0, r0:r1, :] = _layer_norm(h1, ln_g_ref[...], ln_b_ref[...])


def _ffn_kernel(h_ref, w1_ref, w2_ref, ln_g_ref, ln_b_ref, o_ref, *, alpha):
    h = h_ref[...]
    hb = h.astype(BF16)
    acc = alpha * h
    n_ff = D_FF // D_MODEL
    for j in range(n_ff):
        a = jnp.maximum(_dot(hb, w1_ref[:, j * D_MODEL:(j + 1) * D_MODEL]), 0.0)
        a = (a * a).astype(BF16)
        w2_j = w2_ref[j * D_MODEL:(j + 1) * D_MODEL, :]
        if j + 1 < n_ff:
            acc = acc + _dot(a, w2_j)
        else:
            rows = FFN_BLOCK // FFN_TAIL_PIECES
            for r0 in range(0, FFN_BLOCK, rows):
                out = acc[r0:r0 + rows] + _dot(a[r0:r0 + rows], w2_j)
                o_ref[r0:r0 + rows, :] = _layer_norm(out, ln_g_ref[...], ln_b_ref[...])


def _resident(arr):
    nd = arr.ndim
    single = arr.size * arr.dtype.itemsize >= SINGLE_BUFFER_BYTES
    return pl.BlockSpec(arr.shape, lambda *_: (0,) * nd, pipeline_mode=pl.Buffered(1) if single else None)


def _mixer_call(x, w1, w2, consts, alpha):
    bsz, t, d = x.shape
    blocks_per_row = t // MIX_BLOCK
    n_steps = bsz * blocks_per_row
    slab1, slab2 = w1.shape[0] // n_steps, w2.shape[0] // n_steps

    def slab(b, i):
        return (b * blocks_per_row + i, 0)

    return pl.pallas_call(
        functools.partial(_mixer_kernel, alpha=alpha),
        out_shape=(jax.ShapeDtypeStruct((bsz, t, d), F32),
                   jax.ShapeDtypeStruct(w1.shape, BF16), jax.ShapeDtypeStruct(w2.shape, BF16)),
        grid=(bsz, blocks_per_row),
        in_specs=[pl.BlockSpec((1, MIX_BLOCK, d), lambda b, i: (b, i, 0)),
                  pl.BlockSpec((slab1, w1.shape[1]), slab), pl.BlockSpec((slab2, w2.shape[1]), slab)]
        + [pl.BlockSpec(memory_space=pl.ANY)] + [_resident(c) for c in consts[1:]],
        out_specs=(pl.BlockSpec((1, MIX_BLOCK, d), lambda b, i: (b, i, 0)),
                   pl.BlockSpec((slab1, w1.shape[1]), slab), pl.BlockSpec((slab2, w2.shape[1]), slab)),
        scratch_shapes=[pltpu.VMEM((GLA_HEADS, GLA_HEAD_K, GLA_HEAD_V), F32),
                        pltpu.VMEM((d, D_MAIN), BF16),
                        pltpu.VMEM((d, d), BF16),
                        pltpu.VMEM((2, W_STAGE_ROWS, d), F32),
                        pltpu.SemaphoreType.DMA((2,))],
        compiler_params=pltpu.CompilerParams(
            dimension_semantics=("arbitrary", "arbitrary"), vmem_limit_bytes=VMEM_LIMIT),
        name="mixer",
    )(x, w1, w2, *consts)


def _ffn_call(h, w1, w2, ln_g, ln_b, alpha):
    n, d = h.shape
    return pl.pallas_call(
        functools.partial(_ffn_kernel, alpha=alpha),
        out_shape=jax.ShapeDtypeStruct((n, d), F32),
        grid=(n // FFN_BLOCK,),
        in_specs=[pl.BlockSpec((FFN_BLOCK, d), lambda i: (i, 0)),
                  _resident(w1), _resident(w2), _resident(ln_g), _resident(ln_b)],
        out_specs=pl.BlockSpec((FFN_BLOCK, d), lambda i: (i, 0)),
        compiler_params=pltpu.CompilerParams(
            dimension_semantics=("arbitrary",), vmem_limit_bytes=VMEM_LIMIT),
        name="ffn",
    )(h, w1, w2, ln_g, ln_b)


def kernel(x, w_in, b_in, sg_ln_g, sg_ln_b, sg_w_s, sg_b_s, gla_w_gate2, gla_b_gate, gla_norm_g,
           w_out, ln1_g, ln1_b, w_ff1, w_ff2, ln2_g, ln2_b):
    bsz, t, d = x.shape
    depth = w_in.shape[0]
    alpha = (2.0 * depth) ** 0.25
    table_np, level_np = _decay_tables()
    table = jnp.asarray(np.concatenate([table_np, table_np], axis=1), BF16)
    level = jnp.asarray(np.concatenate([level_np, level_np], axis=1))
    upper = jnp.asarray(np.stack([np.broadcast_to(((np.arange(GLA_CHUNK) >> j) & 1)[:, None],
                                                  (GLA_CHUNK, GLA_KEY_DIM)) for j in range(3)]).astype(np.int32))
    low0, low1 = OFF_LOW, OFF_LOW + GLA_GATE_RANK
    rank = GLA_GATE_RANK
    h = x
    for l in range(depth):
        w_main = jnp.transpose(w_in[l])
        b_main = jnp.concatenate([b_in[l][:low0], b_in[l][low1:]])[None, :]
        w_low = jnp.pad(jnp.tile(jnp.transpose(w_main[low0:low1]).astype(BF16), (1, 3)),
                        ((0, 0), (0, 2 * LANES - 3 * rank)))
        b_low = jnp.pad(jnp.tile(b_in[l][low0:low1], 3), (0, 2 * LANES - 3 * rank))[None, :]
        wg_hi = gla_w_gate2[l].astype(BF16)
        wg_lo = (gla_w_gate2[l] - wg_hi.astype(F32)).astype(BF16)
        wg = jnp.pad(jnp.concatenate([wg_hi, wg_hi, wg_lo], axis=0), ((0, LANES - 3 * rank), (0, 0)))
        bs_plane = jnp.repeat(jnp.transpose(sg_b_s[l]), D_MODEL // SG_GROUPS, axis=1)
        consts = (w_main, b_main, w_low, b_low, wg, gla_b_gate[l][None, :],
                  sg_ln_g[l][None, :], sg_ln_b[l][None, :], sg_w_s[l], bs_plane, table, level, upper,
                  gla_norm_g[l][None, :], w_out[l], ln1_g[l][None, :], ln1_b[l][None, :])
        h, w1b, w2b = _mixer_call(h, w_ff1[l], w_ff2[l], consts, alpha)
        h = _ffn_call(h.reshape(bsz * t, d), w1b, w2b,
                      ln2_g[l][None, :], ln2_b[l][None, :], alpha).reshape(bsz, t, d)
    return h
```

```python
import functools
import math

import jax
import jax.numpy as jnp
import numpy as np
from jax.experimental import pallas as pl
from jax.experimental.pallas import tpu as pltpu

D_MODEL = 1024
SG_CHUNK = 128
SG_GROUPS = 8
GLA_HEADS = 4
GLA_KEY_DIM = 512
GLA_VAL_DIM = 1024
GLA_HEAD_K = GLA_KEY_DIM // GLA_HEADS
GLA_HEAD_V = GLA_VAL_DIM // GLA_HEADS
GLA_GATE_RANK = 16
GLA_GATE_TEMP = 16.0
D_FF = 4 * D_MODEL
LN_EPS = 1e-5

LANES = 128
SUBLANES = 8
BF16_ROWS = 16
GLA_CHUNK = 128
GLA_LEVELS = int(math.log2(GLA_CHUNK))
FINE_LEVELS = 3
MIX_BLOCK = 512
FFN_BLOCK = 1024
MIX_TAIL_ROWS = MIX_BLOCK // 2
FFN_TAIL_PIECES = 4
W_STAGE_ROWS = 512
VMEM_LIMIT = 56 * 1024 * 1024

OFF_U, OFF_V, OFF_Q, OFF_K, OFF_GV, OFF_R, OFF_LOW, OFF_GA, OFF_GB = 0, 1024, 2048, 2560, 3072, 4096, 5120, 5136, 6160
D_MAIN = 7168

GELU_C = 0.7978845608028654
LOG2E = 1.4426950408889634

BF16 = jnp.bfloat16
F32 = jnp.float32


def _decay_tables():
    c = GLA_CHUNK
    table = np.tril(np.ones((c, c), np.float32))
    t = np.arange(c)[:, None]
    s = np.arange(c)[None, :]
    x = t ^ s
    top = np.zeros_like(x)
    nz = x > 0
    top[nz] = np.floor(np.log2(x[nz])).astype(x.dtype)
    level = np.where(t > s, GLA_LEVELS - 1 - top, np.where(t == s, GLA_LEVELS, -1)).astype(np.int32)
    return table, level


def _layer_norm(x, g, b):
    mu = jnp.mean(x, axis=-1, keepdims=True)
    xc = x - mu
    var = jnp.mean(xc * xc, axis=-1, keepdims=True)
    return xc * jax.lax.rsqrt(var + LN_EPS) * g + b


def _gelu(x):
    hx = 0.5 * x
    return hx + hx * jnp.tanh(x * (GELU_C + (GELU_C * 0.044715) * (x * x)))


def _sigmoid(x):
    return 0.5 * jnp.tanh(0.5 * x) + 0.5


def _split_bf16(x):
    hi = x.astype(BF16)
    lo = (x - hi.astype(F32)).astype(BF16)
    return hi, lo


def _dot(a, b):
    return jnp.dot(a, b, preferred_element_type=F32)


def _dot_tn(a, b):
    return jax.lax.dot_general(a, b, (((0,), (0,)), ((), ())), preferred_element_type=F32)


def _cast_projection_weights(w_hbm, w_ref, stage_ref, sem):
    starts = [r for r in range(0, D_MAIN + GLA_GATE_RANK, W_STAGE_ROWS) if r < OFF_LOW]
    starts += list(range(OFF_LOW + GLA_GATE_RANK, D_MAIN + GLA_GATE_RANK, W_STAGE_ROWS))

    def copy(j):
        return pltpu.make_async_copy(w_hbm.at[pl.ds(starts[j], W_STAGE_ROWS)], stage_ref.at[j % 2], sem.at[j % 2])

    copy(0).start()
    for j, r0 in enumerate(starts):
        if j + 1 < len(starts):
            copy(j + 1).start()
        copy(j).wait()
        c0 = r0 if r0 < OFF_LOW else r0 - GLA_GATE_RANK
        w_ref[:, c0:c0 + W_STAGE_ROWS] = jnp.transpose(stage_ref[j % 2]).astype(BF16)


def _mixer_kernel(x_ref, w1f_ref, w2f_ref, w_hbm, b_ref, wlow_ref, blow_ref, wg_ref, bg_ref,
                  sg_g_ref, sg_b_ref, ws_ref, bs_ref, tab_ref, lvl_ref, upper_ref, gn_ref,
                  woutf_ref, ln_g_ref, ln_b_ref, o_ref, w1b_ref, w2b_ref,
                  state_ref, w_ref, wout_ref, stage_ref, sem, *, alpha):
    @pl.when((pl.program_id(0) == 0) & (pl.program_id(1) == 0))
    def _():
        _cast_projection_weights(w_hbm, w_ref, stage_ref, sem)
        wout_ref[...] = woutf_ref[...].astype(BF16)

    @pl.when(pl.program_id(1) == 0)
    def _():
        state_ref[...] = jnp.zeros_like(state_ref)

    w1b_ref[...] = w1f_ref[...].astype(BF16)
    w2b_ref[...] = w2f_ref[...].astype(BF16)

    x = x_ref[0]
    xh = x.astype(BF16)
    n_chunks = MIX_BLOCK // GLA_CHUNK

    def proj(off, width):
        c0 = off if off < OFF_LOW else off - GLA_GATE_RANK
        return _dot(xh, w_ref[:, c0:c0 + width]) + b_ref[:, c0:c0 + width]

    pv = proj(OFF_V, D_MODEL)
    q = proj(OFF_Q, GLA_KEY_DIM) * (GLA_HEAD_K ** -0.5)
    k = proj(OFF_K, GLA_KEY_DIM)
    a3 = (_dot(xh, wlow_ref[...]) + blow_ref[...])[:, :LANES]

    v_sg = _layer_norm(_gelu(pv), sg_g_ref[...], sg_b_ref[...]).astype(BF16)
    a3_hi = a3.astype(BF16).astype(F32)
    lane = jax.lax.broadcasted_iota(jnp.int32, a3.shape, 1)
    mid = (lane >= GLA_GATE_RANK) & (lane < 2 * GLA_GATE_RANK)
    z = _dot(jnp.where(mid, a3 - a3_hi, a3_hi).astype(BF16), wg_ref[...]) + bg_ref[...]
    pu = proj(OFF_U, D_MODEL)
    log_f = (jnp.minimum(z, 0.0) - jnp.log(1.0 + jnp.exp(-jnp.abs(z)))) * (LOG2E / GLA_GATE_TEMP)

    row = jax.lax.broadcasted_iota(jnp.int32, (SG_CHUNK, SG_CHUNK), 0)
    col = jax.lax.broadcasted_iota(jnp.int32, (SG_CHUNK, SG_CHUNK), 1)
    mixed_g = []
    for g in range(SG_GROUPS):
        w_g = jnp.where(row >= col, ws_ref[g], 0.0).astype(BF16)
        gs = slice(g * LANES, (g + 1) * LANES)
        rhs = jnp.concatenate([v_sg[c * SG_CHUNK:(c + 1) * SG_CHUNK, gs] for c in range(MIX_BLOCK // SG_CHUNK)],
                              axis=1)
        mixed_g.append(_dot(w_g, rhs))
    mixed = jnp.concatenate(
        [jnp.concatenate([mg[:, c * LANES:(c + 1) * LANES] for mg in mixed_g], axis=1) + bs_ref[...]
         for c in range(MIX_BLOCK // SG_CHUNK)], axis=0)
    pga = proj(OFF_GA, D_MODEL)

    decays, prefixes, suffixes = [], [], []
    for c in range(n_chunks):
        f_hi, f_lo = _split_bf16(log_f[c * GLA_CHUNK:(c + 1) * GLA_CHUNK])
        cum = _dot(tab_ref[...], jnp.concatenate([f_hi, f_lo], axis=0))
        tiles = cum.reshape(GLA_CHUNK // SUBLANES, SUBLANES, GLA_KEY_DIM)
        sub = jax.lax.broadcasted_iota(jnp.int32, tiles.shape, 1)
        level_expo = []
        for lev in range(GLA_LEVELS):
            w = GLA_CHUNK >> (lev + 1)
            if w >= SUBLANES:
                mid = jnp.concatenate([jnp.broadcast_to(cum[b + w - 1:b + w], (2 * w, GLA_KEY_DIM))
                                       for b in range(0, GLA_CHUNK, 2 * w)], axis=0)
            else:
                picks = [jnp.broadcast_to(tiles[:, b + w - 1:b + w, :], tiles.shape) for b in range(0, SUBLANES, 2 * w)]
                mid3 = picks[-1]
                for i in range(len(picks) - 2, -1, -1):
                    mid3 = jnp.where(sub < (i + 1) * 2 * w, picks[i], mid3)
                mid = mid3.reshape(GLA_CHUNK, GLA_KEY_DIM)
            level_expo.append(-jnp.abs(cum - mid))
        decays.append(jnp.exp2(jnp.concatenate(level_expo, axis=0)).astype(BF16))
        prefixes.append(jnp.exp2(cum))
        suffixes.append(jnp.exp2(cum[GLA_CHUNK - 1:GLA_CHUNK] - cum))

    lvl2 = lvl_ref[...]
    pair_w = 2 * GLA_HEAD_K
    no_keys = jnp.zeros((GLA_HEAD_K, GLA_CHUNK), BF16)

    def pair_scores(lhs, keys_t):
        out = []
        for p in range(GLA_HEADS // 2):
            even = keys_t[(2 * p) * GLA_HEAD_K:(2 * p + 1) * GLA_HEAD_K]
            odd = keys_t[(2 * p + 1) * GLA_HEAD_K:(2 * p + 2) * GLA_HEAD_K]
            bd = jnp.concatenate([jnp.concatenate([even, no_keys], axis=1),
                                  jnp.concatenate([no_keys, odd], axis=1)], axis=0)
            out.append(_dot(lhs[:, p * pair_w:(p + 1) * pair_w], bd))
        return out

    def level_update(scores, c, lev):
        qc = q[c * GLA_CHUNK:(c + 1) * GLA_CHUNK]
        kc = k[c * GLA_CHUNK:(c + 1) * GLA_CHUNK]
        w = GLA_CHUNK >> (lev + 1)
        n_blk = GLA_CHUNK // w
        if w >= 8:
            mixed_qk = jnp.concatenate([(qc if (b % 2) else kc)[b * w:(b + 1) * w] for b in range(n_blk)], axis=0)
        else:
            mixed_qk = jnp.where(upper_ref[GLA_LEVELS - 1 - lev] != 0, qc, kc)
        zl = mixed_qk.astype(BF16) * decays[c][lev * GLA_CHUNK:(lev + 1) * GLA_CHUNK]
        zl_t = jnp.transpose(zl)
        if w >= BF16_ROWS:
            lhs = jnp.concatenate([zl[b * w:(b + 1) * w] for b in range(1, n_blk, 2)], axis=0)
            parts = pair_scores(lhs, zl_t)
            out = []
            for p in range(GLA_HEADS // 2):
                rows = []
                for b in range(n_blk):
                    old = scores[p][b * w:(b + 1) * w]
                    if b % 2:
                        new = parts[p][(b // 2) * w:(b // 2 + 1) * w]
                        old = jnp.where(lvl2[b * w:(b + 1) * w] == lev, new, old)
                    rows.append(old)
                out.append(jnp.concatenate(rows, axis=0))
            return out
        parts = pair_scores(zl, zl_t)
        return [jnp.where(lvl2 == lev, parts[p], scores[p]) for p in range(GLA_HEADS // 2)]

    def diagonal_update(scores, c):
        qk = q[c * GLA_CHUNK:(c + 1) * GLA_CHUNK] * k[c * GLA_CHUNK:(c + 1) * GLA_CHUNK]
        out = []
        for p in range(GLA_HEADS // 2):
            diag = [jnp.broadcast_to(jnp.sum(qk[:, h * GLA_HEAD_K:(h + 1) * GLA_HEAD_K], axis=1, keepdims=True),
                                     (GLA_CHUNK, GLA_CHUNK)) for h in (2 * p, 2 * p + 1)]
            out.append(jnp.where(lvl2 == GLA_LEVELS, jnp.concatenate(diag, axis=1), scores[p]))
        return out

    all_scores = [[jnp.zeros((GLA_CHUNK, 2 * GLA_CHUNK), F32) for _ in range(GLA_HEADS // 2)]
                  for _ in range(n_chunks)]
    m_acc = _sigmoid(pga) * (_gelu(pu) * mixed)
    for lev in range(GLA_LEVELS):
        all_scores = [level_update(all_scores[c], c, lev) for c in range(n_chunks)]
        if lev == 1:
            gv = proj(OFF_GV, GLA_VAL_DIM).astype(BF16)
        if lev == 4:
            r = proj(OFF_R, GLA_VAL_DIM)
    all_scores = [diagonal_update(all_scores[c], c) for c in range(n_chunks)]

    heads_sl = [slice(h * GLA_HEAD_K, (h + 1) * GLA_HEAD_K) for h in range(GLA_HEADS)]
    v_heads = [[gv[c * GLA_CHUNK:(c + 1) * GLA_CHUNK, h * GLA_HEAD_V:(h + 1) * GLA_HEAD_V]
                for h in range(GLA_HEADS)] for c in range(n_chunks)]
    kv, scale, q_ins = [], [], []
    for c in range(n_chunks):
        r0 = c * GLA_CHUNK
        q_ins.append((q[r0:r0 + GLA_CHUNK] * prefixes[c]).astype(BF16))
        k_out = (k[r0:r0 + GLA_CHUNK] * suffixes[c]).astype(BF16)
        total = prefixes[c][GLA_CHUNK - 1:GLA_CHUNK]
        kv.append([_dot_tn(k_out[:, sl], v_heads[c][h]) for h, sl in enumerate(heads_sl)])
        tcols = [jnp.transpose(jnp.broadcast_to(total[:, sl], (GLA_HEAD_K, GLA_HEAD_K))) for sl in heads_sl]
        scale.append([jnp.concatenate([t, t], axis=1) for t in tcols])
    states = [state_ref[h] for h in range(GLA_HEADS)]
    o_rows = []
    for c in range(n_chunks):
        heads = []
        for h, sl in enumerate(heads_sl):
            s_h = all_scores[c][h // 2][:, (h % 2) * GLA_CHUNK:(h % 2 + 1) * GLA_CHUNK].astype(BF16)
            o_h = _dot(jnp.concatenate([s_h, q_ins[c][:, sl]], axis=1),
                       jnp.concatenate([v_heads[c][h], states[h].astype(BF16)], axis=0))
            states[h] = states[h] * scale[c][h] + kv[c][h]
            o_h = o_h * jax.lax.rsqrt(jnp.mean(o_h * o_h, axis=-1, keepdims=True) + LN_EPS)
            heads.append(o_h)
        o_rows.append(jnp.concatenate(heads, axis=1))
    for h in range(GLA_HEADS):
        state_ref[h] = states[h]
    pgb = proj(OFF_GB, D_MODEL)
    y_b = (jnp.concatenate(o_rows, axis=0) * gn_ref[...]) * (r * _sigmoid(r))
    m = (m_acc + _sigmoid(pgb) * y_b).astype(BF16)

    for r0, r1 in ((0, MIX_BLOCK - MIX_TAIL_ROWS), (MIX_BLOCK - MIX_TAIL_ROWS, MIX_BLOCK)):
        h1 = alpha * x_ref[0, r0:r1, :] + _dot(m[r0:r1], wout_ref[...])
        o_ref[0, r0:r1, :] = _layer_norm(h1, ln_g_ref[...], ln_b_ref[...])


def _ffn_kernel(h_ref, w1_ref, w2_ref, ln_g_ref, ln_b_ref, o_ref, *, alpha):
    h = h_ref[...]
    hb = h.astype(BF16)
    acc = alpha * h
    n_ff = D_FF // D_MODEL
    for j in range(n_ff):
        a = jnp.maximum(_dot(hb, w1_ref[:, j * D_MODEL:(j + 1) * D_MODEL]), 0.0)
        a = (a * a).astype(BF16)
        w2_j = w2_ref[j * D_MODEL:(j + 1) * D_MODEL, :]
        if j + 1 < n_ff:
            acc = acc + _dot(a, w2_j)
        else:
            rows = FFN_BLOCK // FFN_TAIL_PIECES
            for r0 in range(0, FFN_BLOCK, rows):
                out = acc[r0:r0 + rows] + _dot(a[r0:r0 + rows], w2_j)
                o_ref[r0:r0 + rows, :] = _layer_norm(out, ln_g_ref[...], ln_b_ref[...])


def _resident(shape):
    nd = len(shape)
    return pl.BlockSpec(shape, lambda *_: (0,) * nd, pipeline_mode=pl.Buffered(1))


def _mixer_call(x, w1, w2, consts, alpha):
    bsz, t, d = x.shape
    blocks_per_row = t // MIX_BLOCK
    n_steps = bsz * blocks_per_row
    slab1, slab2 = w1.shape[0] // n_steps, w2.shape[0] // n_steps

    def slab(b, i):
        return (b * blocks_per_row + i, 0)

    return pl.pallas_call(
        functools.partial(_mixer_kernel, alpha=alpha),
        out_shape=(jax.ShapeDtypeStruct((bsz, t, d), F32),
                   jax.ShapeDtypeStruct(w1.shape, BF16), jax.ShapeDtypeStruct(w2.shape, BF16)),
        grid=(bsz, blocks_per_row),
        in_specs=[pl.BlockSpec((1, MIX_BLOCK, d), lambda b, i: (b, i, 0)),
                  pl.BlockSpec((slab1, w1.shape[1]), slab), pl.BlockSpec((slab2, w2.shape[1]), slab)]
        + [pl.BlockSpec(memory_space=pl.ANY)] + [_resident(c.shape) for c in consts[1:]],
        out_specs=(pl.BlockSpec((1, MIX_BLOCK, d), lambda b, i: (b, i, 0)),
                   pl.BlockSpec((slab1, w1.shape[1]), slab), pl.BlockSpec((slab2, w2.shape[1]), slab)),
        scratch_shapes=[pltpu.VMEM((GLA_HEADS, GLA_HEAD_K, GLA_HEAD_V), F32),
                        pltpu.VMEM((d, D_MAIN), BF16),
                        pltpu.VMEM((d, d), BF16),
                        pltpu.VMEM((2, W_STAGE_ROWS, d), F32),
                        pltpu.SemaphoreType.DMA((2,))],
        compiler_params=pltpu.CompilerParams(
            dimension_semantics=("arbitrary", "arbitrary"), vmem_limit_bytes=VMEM_LIMIT),
        name="mixer",
    )(x, w1, w2, *consts)


def _ffn_call(h, w1, w2, ln_g, ln_b, alpha):
    n, d = h.shape
    return pl.pallas_call(
        functools.partial(_ffn_kernel, alpha=alpha),
        out_shape=jax.ShapeDtypeStruct((n, d), F32),
        grid=(n // FFN_BLOCK,),
        in_specs=[pl.BlockSpec((FFN_BLOCK, d), lambda i: (i, 0)),
                  _resident(w1.shape), _resident(w2.shape), _resident(ln_g.shape), _resident(ln_b.shape)],
        out_specs=pl.BlockSpec((FFN_BLOCK, d), lambda i: (i, 0)),
        compiler_params=pltpu.CompilerParams(
            dimension_semantics=("arbitrary",), vmem_limit_bytes=VMEM_LIMIT),
        name="ffn",
    )(h, w1, w2, ln_g, ln_b)


def kernel(x, w_in, b_in, sg_ln_g, sg_ln_b, sg_w_s, sg_b_s, gla_w_gate2, gla_b_gate, gla_norm_g,
           w_out, ln1_g, ln1_b, w_ff1, w_ff2, ln2_g, ln2_b):
    bsz, t, d = x.shape
    depth = w_in.shape[0]
    alpha = (2.0 * depth) ** 0.25
    table_np, level_np = _decay_tables()
    table = jnp.asarray(np.concatenate([table_np, table_np], axis=1), BF16)
    level = jnp.asarray(np.concatenate([level_np, level_np], axis=1))
    upper = jnp.asarray(np.stack([np.broadcast_to(((np.arange(GLA_CHUNK) >> j) & 1)[:, None],
                                                  (GLA_CHUNK, GLA_KEY_DIM)) for j in range(FINE_LEVELS)]).astype(np.int32))
    low0, low1 = OFF_LOW, OFF_LOW + GLA_GATE_RANK
    rank = GLA_GATE_RANK
    h = x
    for l in range(depth):
        w_main = jnp.transpose(w_in[l])
        b_main = jnp.concatenate([b_in[l][:low0], b_in[l][low1:]])[None, :]
        w_low = jnp.pad(jnp.tile(jnp.transpose(w_main[low0:low1]).astype(BF16), (1, 3)),
                        ((0, 0), (0, 2 * LANES - 3 * rank)))
        b_low = jnp.pad(jnp.tile(b_in[l][low0:low1], 3), (0, 2 * LANES - 3 * rank))[None, :]
        wg_hi = gla_w_gate2[l].astype(BF16)
        wg_lo = (gla_w_gate2[l] - wg_hi.astype(F32)).astype(BF16)
        wg = jnp.pad(jnp.concatenate([wg_hi, wg_hi, wg_lo], axis=0), ((0, LANES - 3 * rank), (0, 0)))
        bs_plane = jnp.repeat(jnp.transpose(sg_b_s[l]), D_MODEL // SG_GROUPS, axis=1)
        consts = (w_main, b_main, w_low, b_low, wg, gla_b_gate[l][None, :],
                  sg_ln_g[l][None, :], sg_ln_b[l][None, :], sg_w_s[l], bs_plane, table, level, upper,
                  gla_norm_g[l][None, :], w_out[l], ln1_g[l][None, :], ln1_b[l][None, :])
        h, w1b, w2b = _mixer_call(h, w_ff1[l], w_ff2[l], consts, alpha)
        h = _ffn_call(h.reshape(bsz * t, d), w1b, w2b,
                      ln2_g[l][None, :], ln2_b[l][None, :], alpha).reshape(bsz, t, d)
    return h
```

```python
import functools
import math

import jax
import jax.numpy as jnp
import numpy as np
from jax.experimental import pallas as pl
from jax.experimental.pallas import tpu as pltpu

D_MODEL = 1024
SG_CHUNK = 128
SG_GROUPS = 8
GLA_HEADS = 4
GLA_KEY_DIM = 512
GLA_VAL_DIM = 1024
GLA_HEAD_K = GLA_KEY_DIM // GLA_HEADS
GLA_HEAD_V = GLA_VAL_DIM // GLA_HEADS
GLA_GATE_RANK = 16
GLA_GATE_TEMP = 16.0
D_FF = 4 * D_MODEL
LN_EPS = 1e-5

LANES = 128
BF16_ROWS = 16
GLA_CHUNK = 128
GLA_LEVELS = int(math.log2(GLA_CHUNK))
FINE_LEVELS = 3
MIX_BLOCK = 512
FFN_BLOCK = 1024
MIX_TAIL_ROWS = 256
FFN_TAIL_PIECES = 4
FFN_SLAB = 2048
W_STAGE_ROWS = 512
VMEM_LIMIT = 56 * 1024 * 1024

OFF_U, OFF_V, OFF_Q, OFF_K, OFF_GV, OFF_R, OFF_LOW, OFF_GA, OFF_GB = 0, 1024, 2048, 2560, 3072, 4096, 5120, 5136, 6160
D_MAIN = 7168

GELU_C = 0.7978845608028654
LOG2E = 1.4426950408889634

BF16 = jnp.bfloat16
F32 = jnp.float32


def _decay_tables():
    c = GLA_CHUNK
    blocks = []
    for lw in range(FINE_LEVELS - 1, -1, -1):
        a = np.zeros((c, c), np.float32)
        for t in range(c):
            if (t >> lw) & 1:
                a[t, (t >> lw) << lw:t + 1] = 1.0
            else:
                a[t, t + 1:((t >> lw) + 1) << lw] = 1.0
        blocks.append(a)
    blocks.append(np.tril(np.ones((c, c), np.float32)))
    table = np.concatenate(blocks, axis=0)
    t = np.arange(c)[:, None]
    s = np.arange(c)[None, :]
    x = t ^ s
    top = np.zeros_like(x)
    nz = x > 0
    top[nz] = np.floor(np.log2(x[nz])).astype(x.dtype)
    level = np.where(t > s, GLA_LEVELS - 1 - top, np.where(t == s, GLA_LEVELS, -1)).astype(np.int32)
    return table, level


def _layer_norm(x, g, b):
    mu = jnp.mean(x, axis=-1, keepdims=True)
    xc = x - mu
    var = jnp.mean(xc * xc, axis=-1, keepdims=True)
    return xc * jax.lax.rsqrt(var + LN_EPS) * g + b


def _gelu(x):
    hx = 0.5 * x
    return hx + hx * jnp.tanh(x * (GELU_C + (GELU_C * 0.044715) * (x * x)))


def _sigmoid(x):
    return 0.5 * jnp.tanh(0.5 * x) + 0.5


def _split_bf16(x):
    hi = x.astype(BF16)
    lo = (x - hi.astype(F32)).astype(BF16)
    return hi, lo


def _dot(a, b):
    return jnp.dot(a, b, preferred_element_type=F32)


def _dot_tn(a, b):
    return jax.lax.dot_general(a, b, (((0,), (0,)), ((), ())), preferred_element_type=F32)


def _cast_projection_weights(w_hbm, w_ref, stage_ref, sem):
    starts = [r for r in range(0, D_MAIN + GLA_GATE_RANK, W_STAGE_ROWS) if r < OFF_LOW]
    starts += list(range(OFF_LOW + GLA_GATE_RANK, D_MAIN + GLA_GATE_RANK, W_STAGE_ROWS))

    def copy(j):
        return pltpu.make_async_copy(w_hbm.at[pl.ds(starts[j], W_STAGE_ROWS)], stage_ref.at[j % 2], sem.at[j % 2])

    copy(0).start()
    for j, r0 in enumerate(starts):
        if j + 1 < len(starts):
            copy(j + 1).start()
        copy(j).wait()
        c0 = r0 if r0 < OFF_LOW else r0 - GLA_GATE_RANK
        w_ref[:, c0:c0 + W_STAGE_ROWS] = jnp.transpose(stage_ref[j % 2]).astype(BF16)


def _mixer_kernel(x_ref, w1f_ref, w2f_ref, w_hbm, b_ref, wlow_ref, blow_ref, wg_ref, bg_ref,
                  sg_g_ref, sg_b_ref, ws_ref, bs_ref, tab_ref, lvl_ref, upper_ref, gn_ref,
                  woutf_ref, ln_g_ref, ln_b_ref, o_ref, w1b_ref, w2b_ref,
                  state_ref, w_ref, wout_ref, stage_ref, sem, *, alpha):
    @pl.when((pl.program_id(0) == 0) & (pl.program_id(1) == 0))
    def _():
        _cast_projection_weights(w_hbm, w_ref, stage_ref, sem)
        wout_ref[...] = woutf_ref[...].astype(BF16)

    @pl.when(pl.program_id(1) == 0)
    def _():
        state_ref[...] = jnp.zeros_like(state_ref)

    w1b_ref[...] = w1f_ref[...].astype(BF16)
    w2b_ref[...] = w2f_ref[...].astype(BF16)

    x = x_ref[0]
    xh = x.astype(BF16)
    n_chunks = MIX_BLOCK // GLA_CHUNK

    def proj(off, width):
        c0 = off if off < OFF_LOW else off - GLA_GATE_RANK
        return _dot(xh, w_ref[:, c0:c0 + width]) + b_ref[:, c0:c0 + width]

    pv = proj(OFF_V, D_MODEL)
    q = proj(OFF_Q, GLA_KEY_DIM) * (GLA_HEAD_K ** -0.5)
    k = proj(OFF_K, GLA_KEY_DIM)
    a3 = (_dot(xh, wlow_ref[...]) + blow_ref[...])[:, :LANES]

    v_sg = _layer_norm(_gelu(pv), sg_g_ref[...], sg_b_ref[...]).astype(BF16)
    a3_hi = a3.astype(BF16).astype(F32)
    lane = jax.lax.broadcasted_iota(jnp.int32, a3.shape, 1)
    mid = (lane >= GLA_GATE_RANK) & (lane < 2 * GLA_GATE_RANK)
    z = _dot(jnp.where(mid, a3 - a3_hi, a3_hi).astype(BF16), wg_ref[...]) + bg_ref[...]
    pu = proj(OFF_U, D_MODEL)
    log_f = (jnp.minimum(z, 0.0) - jnp.log(1.0 + jnp.exp(-jnp.abs(z)))) * (LOG2E / GLA_GATE_TEMP)

    row = jax.lax.broadcasted_iota(jnp.int32, (SG_CHUNK, SG_CHUNK), 0)
    col = jax.lax.broadcasted_iota(jnp.int32, (SG_CHUNK, SG_CHUNK), 1)
    mixed_g = []
    for g in range(SG_GROUPS):
        w_g = jnp.where(row >= col, ws_ref[g], 0.0).astype(BF16)
        gs = slice(g * LANES, (g + 1) * LANES)
        rhs = jnp.concatenate([v_sg[c * SG_CHUNK:(c + 1) * SG_CHUNK, gs] for c in range(MIX_BLOCK // SG_CHUNK)],
                              axis=1)
        mixed_g.append(_dot(w_g, rhs))
    mixed = jnp.concatenate(
        [jnp.concatenate([mg[:, c * LANES:(c + 1) * LANES] for mg in mixed_g], axis=1) + bs_ref[...]
         for c in range(MIX_BLOCK // SG_CHUNK)], axis=0)
    pga = proj(OFF_GA, D_MODEL)

    decays, prefixes, suffixes = [], [], []
    for c in range(n_chunks):
        f_hi, f_lo = _split_bf16(log_f[c * GLA_CHUNK:(c + 1) * GLA_CHUNK])
        expo = _dot(tab_ref[...], jnp.concatenate([f_hi, f_lo], axis=0))
        cum = expo[FINE_LEVELS * GLA_CHUNK:]
        level_expo = []
        for lev in range(GLA_LEVELS - FINE_LEVELS):
            w = GLA_CHUNK >> (lev + 1)
            mid = jnp.concatenate([jnp.broadcast_to(cum[b + w - 1:b + w], (2 * w, GLA_KEY_DIM))
                                   for b in range(0, GLA_CHUNK, 2 * w)], axis=0)
            level_expo.append(-jnp.abs(cum - mid))
        level_expo.append(expo[:FINE_LEVELS * GLA_CHUNK])
        decays.append(jnp.exp2(jnp.concatenate(level_expo, axis=0)).astype(BF16))
        prefixes.append(jnp.exp2(cum))
        suffixes.append(jnp.exp2(cum[GLA_CHUNK - 1:GLA_CHUNK] - cum))

    lvl2 = lvl_ref[...]
    pair_w = 2 * GLA_HEAD_K
    no_keys = jnp.zeros((GLA_HEAD_K, GLA_CHUNK), BF16)

    def pair_scores(lhs, keys_t):
        out = []
        for p in range(GLA_HEADS // 2):
            even = keys_t[(2 * p) * GLA_HEAD_K:(2 * p + 1) * GLA_HEAD_K]
            odd = keys_t[(2 * p + 1) * GLA_HEAD_K:(2 * p + 2) * GLA_HEAD_K]
            bd = jnp.concatenate([jnp.concatenate([even, no_keys], axis=1),
                                  jnp.concatenate([no_keys, odd], axis=1)], axis=0)
            out.append(_dot(lhs[:, p * pair_w:(p + 1) * pair_w], bd))
        return out

    def level_update(scores, c, lev):
        qc = q[c * GLA_CHUNK:(c + 1) * GLA_CHUNK]
        kc = k[c * GLA_CHUNK:(c + 1) * GLA_CHUNK]
        w = GLA_CHUNK >> (lev + 1)
        n_blk = GLA_CHUNK // w
        if w >= 8:
            mixed_qk = jnp.concatenate([(qc if (b % 2) else kc)[b * w:(b + 1) * w] for b in range(n_blk)], axis=0)
        else:
            mixed_qk = jnp.where(upper_ref[GLA_LEVELS - 1 - lev] != 0, qc, kc)
        zl = mixed_qk.astype(BF16) * decays[c][lev * GLA_CHUNK:(lev + 1) * GLA_CHUNK]
        zl_t = jnp.transpose(zl)
        if w >= BF16_ROWS:
            lhs = jnp.concatenate([zl[b * w:(b + 1) * w] for b in range(1, n_blk, 2)], axis=0)
            parts = pair_scores(lhs, zl_t)
            out = []
            for p in range(GLA_HEADS // 2):
                rows = []
                for b in range(n_blk):
                    old = scores[p][b * w:(b + 1) * w]
                    if b % 2:
                        new = parts[p][(b // 2) * w:(b // 2 + 1) * w]
                        old = jnp.where(lvl2[b * w:(b + 1) * w] == lev, new, old)
                    rows.append(old)
                out.append(jnp.concatenate(rows, axis=0))
            return out
        parts = pair_scores(zl, zl_t)
        return [jnp.where(lvl2 == lev, parts[p], scores[p]) for p in range(GLA_HEADS // 2)]

    def diagonal_update(scores, c):
        qk = q[c * GLA_CHUNK:(c + 1) * GLA_CHUNK] * k[c * GLA_CHUNK:(c + 1) * GLA_CHUNK]
        out = []
        for p in range(GLA_HEADS // 2):
            diag = [jnp.broadcast_to(jnp.sum(qk[:, h * GLA_HEAD_K:(h + 1) * GLA_HEAD_K], axis=1, keepdims=True),
                                     (GLA_CHUNK, GLA_CHUNK)) for h in (2 * p, 2 * p + 1)]
            out.append(jnp.where(lvl2 == GLA_LEVELS, jnp.concatenate(diag, axis=1), scores[p]))
        return out

    all_scores = [[jnp.zeros((GLA_CHUNK, 2 * GLA_CHUNK), F32) for _ in range(GLA_HEADS // 2)]
                  for _ in range(n_chunks)]
    m_acc = _sigmoid(pga) * (_gelu(pu) * mixed)
    for lev in range(GLA_LEVELS):
        all_scores = [level_update(all_scores[c], c, lev) for c in range(n_chunks)]
        if lev == 1:
            gv = proj(OFF_GV, GLA_VAL_DIM).astype(BF16)
        if lev == 4:
            r = proj(OFF_R, GLA_VAL_DIM)
    all_scores = [diagonal_update(all_scores[c], c) for c in range(n_chunks)]

    heads_sl = [slice(h * GLA_HEAD_K, (h + 1) * GLA_HEAD_K) for h in range(GLA_HEADS)]
    v_heads = [[gv[c * GLA_CHUNK:(c + 1) * GLA_CHUNK, h * GLA_HEAD_V:(h + 1) * GLA_HEAD_V]
                for h in range(GLA_HEADS)] for c in range(n_chunks)]
    kv, scale, q_ins = [], [], []
    for c in range(n_chunks):
        r0 = c * GLA_CHUNK
        q_ins.append((q[r0:r0 + GLA_CHUNK] * prefixes[c]).astype(BF16))
        k_out = (k[r0:r0 + GLA_CHUNK] * suffixes[c]).astype(BF16)
        total = prefixes[c][GLA_CHUNK - 1:GLA_CHUNK]
        kv.append([_dot_tn(k_out[:, sl], v_heads[c][h]) for h, sl in enumerate(heads_sl)])
        tcols = [jnp.transpose(jnp.broadcast_to(total[:, sl], (GLA_HEAD_K, GLA_HEAD_K))) for sl in heads_sl]
        scale.append([jnp.concatenate([t, t], axis=1) for t in tcols])
    states = [state_ref[h] for h in range(GLA_HEADS)]
    o_rows = []
    for c in range(n_chunks):
        heads = []
        for h, sl in enumerate(heads_sl):
            s_h = all_scores[c][h // 2][:, (h % 2) * GLA_CHUNK:(h % 2 + 1) * GLA_CHUNK].astype(BF16)
            o_h = _dot(jnp.concatenate([s_h, q_ins[c][:, sl]], axis=1),
                       jnp.concatenate([v_heads[c][h], states[h].astype(BF16)], axis=0))
            states[h] = states[h] * scale[c][h] + kv[c][h]
            o_h = o_h * jax.lax.rsqrt(jnp.mean(o_h * o_h, axis=-1, keepdims=True) + LN_EPS)
            heads.append(o_h)
        o_rows.append(jnp.concatenate(heads, axis=1))
    for h in range(GLA_HEADS):
        state_ref[h] = states[h]
    pgb = proj(OFF_GB, D_MODEL)
    y_b = (jnp.concatenate(o_rows, axis=0) * gn_ref[...]) * (r * _sigmoid(r))
    m = (m_acc + _sigmoid(pgb) * y_b).astype(BF16)

    for r0, r1 in ((0, MIX_BLOCK - MIX_TAIL_ROWS), (MIX_BLOCK - MIX_TAIL_ROWS, MIX_BLOCK)):
        h1 = alpha * x_ref[0, r0:r1, :] + _dot(m[r0:r1], wout_ref[...])
        o_ref[0, r0:r1, :] = _layer_norm(h1, ln_g_ref[...], ln_b_ref[...])


def _ffn_kernel(h_ref, w1_ref, w2_ref, ln_g_ref, ln_b_ref, o_ref, *, alpha):
    h = h_ref[...]
    hb = h.astype(BF16)
    acc = alpha * h
    slab = FFN_SLAB

    def up(j):
        a = jnp.maximum(_dot(hb, w1_ref[:, j * slab:(j + 1) * slab]), 0.0)
        return (a * a).astype(BF16)

    n_slabs = D_FF // slab
    for j in range(n_slabs - 1):
        acc = acc + _dot(up(j), w2_ref[j * slab:(j + 1) * slab, :])
    a = up(n_slabs - 1)
    w2_last = w2_ref[(n_slabs - 1) * slab:, :]
    rows = FFN_BLOCK // FFN_TAIL_PIECES
    for r0 in range(0, FFN_BLOCK, rows):
        out = acc[r0:r0 + rows] + _dot(a[r0:r0 + rows], w2_last)
        o_ref[r0:r0 + rows, :] = _layer_norm(out, ln_g_ref[...], ln_b_ref[...])


def _resident(shape):
    nd = len(shape)
    return pl.BlockSpec(shape, lambda *_: (0,) * nd, pipeline_mode=pl.Buffered(1))


def _mixer_call(x, w1, w2, consts, alpha):
    bsz, t, d = x.shape
    blocks_per_row = t // MIX_BLOCK
    n_steps = bsz * blocks_per_row
    slab1, slab2 = w1.shape[0] // n_steps, w2.shape[0] // n_steps

    def slab(b, i):
        return (b * blocks_per_row + i, 0)

    return pl.pallas_call(
        functools.partial(_mixer_kernel, alpha=alpha),
        out_shape=(jax.ShapeDtypeStruct((bsz, t, d), F32),
                   jax.ShapeDtypeStruct(w1.shape, BF16), jax.ShapeDtypeStruct(w2.shape, BF16)),
        grid=(bsz, blocks_per_row),
        in_specs=[pl.BlockSpec((1, MIX_BLOCK, d), lambda b, i: (b, i, 0)),
                  pl.BlockSpec((slab1, w1.shape[1]), slab), pl.BlockSpec((slab2, w2.shape[1]), slab)]
        + [pl.BlockSpec(memory_space=pl.ANY)] + [_resident(c.shape) for c in consts[1:]],
        out_specs=(pl.BlockSpec((1, MIX_BLOCK, d), lambda b, i: (b, i, 0)),
                   pl.BlockSpec((slab1, w1.shape[1]), slab), pl.BlockSpec((slab2, w2.shape[1]), slab)),
        scratch_shapes=[pltpu.VMEM((GLA_HEADS, GLA_HEAD_K, GLA_HEAD_V), F32),
                        pltpu.VMEM((d, D_MAIN), BF16),
                        pltpu.VMEM((d, d), BF16),
                        pltpu.VMEM((2, W_STAGE_ROWS, d), F32),
                        pltpu.SemaphoreType.DMA((2,))],
        compiler_params=pltpu.CompilerParams(
            dimension_semantics=("arbitrary", "arbitrary"), vmem_limit_bytes=VMEM_LIMIT),
        name="mixer",
    )(x, w1, w2, *consts)


def _ffn_call(h, w1, w2, ln_g, ln_b, alpha):
    n, d = h.shape
    return pl.pallas_call(
        functools.partial(_ffn_kernel, alpha=alpha),
        out_shape=jax.ShapeDtypeStruct((n, d), F32),
        grid=(n // FFN_BLOCK,),
        in_specs=[pl.BlockSpec((FFN_BLOCK, d), lambda i: (i, 0)),
                  _resident(w1.shape), _resident(w2.shape), _resident(ln_g.shape), _resident(ln_b.shape)],
        out_specs=pl.BlockSpec((FFN_BLOCK, d), lambda i: (i, 0)),
        compiler_params=pltpu.CompilerParams(
            dimension_semantics=("arbitrary",), vmem_limit_bytes=VMEM_LIMIT),
        name="ffn",
    )(h, w1, w2, ln_g, ln_b)


def kernel(x, w_in, b_in, sg_ln_g, sg_ln_b, sg_w_s, sg_b_s, gla_w_gate2, gla_b_gate, gla_norm_g,
           w_out, ln1_g, ln1_b, w_ff1, w_ff2, ln2_g, ln2_b):
    bsz, t, d = x.shape
    depth = w_in.shape[0]
    alpha = (2.0 * depth) ** 0.25
    table_np, level_np = _decay_tables()
    table = jnp.asarray(np.concatenate([table_np, table_np], axis=1), BF16)
    level = jnp.asarray(np.concatenate([level_np, level_np], axis=1))
    upper = jnp.asarray(np.stack([np.broadcast_to(((np.arange(GLA_CHUNK) >> j) & 1)[:, None],
                                                  (GLA_CHUNK, GLA_KEY_DIM)) for j in range(FINE_LEVELS)]).astype(np.int32))
    low0, low1 = OFF_LOW, OFF_LOW + GLA_GATE_RANK
    rank = GLA_GATE_RANK
    h = x
    for l in range(depth):
        w_main = jnp.transpose(w_in[l])
        b_main = jnp.concatenate([b_in[l][:low0], b_in[l][low1:]])[None, :]
        w_low = jnp.pad(jnp.tile(jnp.transpose(w_main[low0:low1]).astype(BF16), (1, 3)),
                        ((0, 0), (0, 2 * LANES - 3 * rank)))
        b_low = jnp.pad(jnp.tile(b_in[l][low0:low1], 3), (0, 2 * LANES - 3 * rank))[None, :]
        wg_hi = gla_w_gate2[l].astype(BF16)
        wg_lo = (gla_w_gate2[l] - wg_hi.astype(F32)).astype(BF16)
        wg = jnp.pad(jnp.concatenate([wg_hi, wg_hi, wg_lo], axis=0), ((0, LANES - 3 * rank), (0, 0)))
        bs_plane = jnp.repeat(jnp.transpose(sg_b_s[l]), D_MODEL // SG_GROUPS, axis=1)
        consts = (w_main, b_main, w_low, b_low, wg, gla_b_gate[l][None, :],
                  sg_ln_g[l][None, :], sg_ln_b[l][None, :], sg_w_s[l], bs_plane, table, level, upper,
                  gla_norm_g[l][None, :], w_out[l], ln1_g[l][None, :], ln1_b[l][None, :])
        h, w1b, w2b = _mixer_call(h, w_ff1[l], w_ff2[l], consts, alpha)
        h = _ffn_call(h.reshape(bsz * t, d), w1b, w2b,
                      ln2_g[l][None, :], ln2_b[l][None, :], alpha).reshape(bsz, t, d)
    return h
```

```python
import functools
import math

import jax
import jax.numpy as jnp
import numpy as np
from jax.experimental import pallas as pl
from jax.experimental.pallas import tpu as pltpu

D_MODEL = 1024
SG_CHUNK = 128
SG_GROUPS = 8
GLA_HEADS = 4
GLA_KEY_DIM = 512
GLA_VAL_DIM = 1024
GLA_HEAD_K = GLA_KEY_DIM // GLA_HEADS
GLA_HEAD_V = GLA_VAL_DIM // GLA_HEADS
GLA_GATE_RANK = 16
GLA_GATE_TEMP = 16.0
D_FF = 4 * D_MODEL
LN_EPS = 1e-5

LANES = 128
BF16_ROWS = 16
GLA_CHUNK = 128
GLA_LEVELS = int(math.log2(GLA_CHUNK))
FINE_LEVELS = 3
MIX_BLOCK = 512
FFN_BLOCK = 1024
MIX_TAIL_ROWS = 256
FFN_TAIL_PIECES = 4
FFN_SLAB = 2048
W_STAGE_ROWS = 512
VMEM_LIMIT = 56 * 1024 * 1024

OFF_U, OFF_V, OFF_Q, OFF_K, OFF_GV, OFF_R, OFF_LOW, OFF_GA, OFF_GB = 0, 1024, 2048, 2560, 3072, 4096, 5120, 5136, 6160
D_MAIN = 7168
ROW_B_MAIN, ROW_B_LOW, ROW_B_GATE, ROW_SG_G, ROW_SG_B, ROW_GN, ROW_LN_G, ROW_LN_B = (
    0, 7168, 7424, 7936, 8960, 9984, 11008, 12032)

GELU_C = 0.7978845608028654
LOG2E = 1.4426950408889634

BF16 = jnp.bfloat16
F32 = jnp.float32


def _decay_tables():
    c = GLA_CHUNK
    blocks = []
    for lw in range(FINE_LEVELS - 1, -1, -1):
        a = np.zeros((c, c), np.float32)
        for t in range(c):
            if (t >> lw) & 1:
                a[t, (t >> lw) << lw:t + 1] = 1.0
            else:
                a[t, t + 1:((t >> lw) + 1) << lw] = 1.0
        blocks.append(a)
    blocks.append(np.tril(np.ones((c, c), np.float32)))
    table = np.concatenate(blocks, axis=0)
    t = np.arange(c)[:, None]
    s = np.arange(c)[None, :]
    x = t ^ s
    top = np.zeros_like(x)
    nz = x > 0
    top[nz] = np.floor(np.log2(x[nz])).astype(x.dtype)
    level = np.where(t > s, GLA_LEVELS - 1 - top, np.where(t == s, GLA_LEVELS, -1)).astype(np.int32)
    return table, level


def _layer_norm(x, g, b):
    mu = jnp.mean(x, axis=-1, keepdims=True)
    xc = x - mu
    var = jnp.mean(xc * xc, axis=-1, keepdims=True)
    return xc * jax.lax.rsqrt(var + LN_EPS) * g + b


def _gelu(x):
    hx = 0.5 * x
    return hx + hx * jnp.tanh(x * (GELU_C + (GELU_C * 0.044715) * (x * x)))


def _sigmoid(x):
    return 0.5 * jnp.tanh(0.5 * x) + 0.5


def _split_bf16(x):
    hi = x.astype(BF16)
    lo = (x - hi.astype(F32)).astype(BF16)
    return hi, lo


def _dot(a, b):
    return jnp.dot(a, b, preferred_element_type=F32)


def _dot_tn(a, b):
    return jax.lax.dot_general(a, b, (((0,), (0,)), ((), ())), preferred_element_type=F32)


def _cast_projection_weights(w_hbm, w_ref, stage_ref, sem):
    starts = [r for r in range(0, D_MAIN + GLA_GATE_RANK, W_STAGE_ROWS) if r < OFF_LOW]
    starts += list(range(OFF_LOW + GLA_GATE_RANK, D_MAIN + GLA_GATE_RANK, W_STAGE_ROWS))

    def copy(j):
        return pltpu.make_async_copy(w_hbm.at[pl.ds(starts[j], W_STAGE_ROWS)], stage_ref.at[j % 2], sem.at[j % 2])

    copy(0).start()
    for j, r0 in enumerate(starts):
        if j + 1 < len(starts):
            copy(j + 1).start()
        copy(j).wait()
        c0 = r0 if r0 < OFF_LOW else r0 - GLA_GATE_RANK
        w_ref[:, c0:c0 + W_STAGE_ROWS] = jnp.transpose(stage_ref[j % 2]).astype(BF16)


def _mixer_kernel(x_ref, w1f_ref, w2f_ref, w_hbm, rows_ref, wlow_ref, wg_ref,
                  ws_ref, bs_ref, tab_ref, lvl_ref, upper_ref, woutf_ref, o_ref, w1b_ref, w2b_ref,
                  state_ref, w_ref, wout_ref, stage_ref, sem, *, alpha):
    @pl.when((pl.program_id(0) == 0) & (pl.program_id(1) == 0))
    def _():
        _cast_projection_weights(w_hbm, w_ref, stage_ref, sem)
        wout_ref[...] = woutf_ref[...].astype(BF16)

    @pl.when(pl.program_id(1) == 0)
    def _():
        state_ref[...] = jnp.zeros_like(state_ref)

    w1b_ref[...] = w1f_ref[...].astype(BF16)
    w2b_ref[...] = w2f_ref[...].astype(BF16)

    x = x_ref[0]
    xh = x.astype(BF16)
    n_chunks = MIX_BLOCK // GLA_CHUNK

    def param_row(off, width):
        return rows_ref[:, off:off + width]

    def proj(off, width):
        c0 = off if off < OFF_LOW else off - GLA_GATE_RANK
        return _dot(xh, w_ref[:, c0:c0 + width]) + param_row(ROW_B_MAIN + c0, width)

    pv = proj(OFF_V, D_MODEL)
    q = proj(OFF_Q, GLA_KEY_DIM) * (GLA_HEAD_K ** -0.5)
    k = proj(OFF_K, GLA_KEY_DIM)
    a3 = (_dot(xh, wlow_ref[...]) + param_row(ROW_B_LOW, 2 * LANES))[:, :LANES]

    v_sg = _layer_norm(_gelu(pv), param_row(ROW_SG_G, D_MODEL), param_row(ROW_SG_B, D_MODEL)).astype(BF16)
    a3_hi = a3.astype(BF16).astype(F32)
    lane = jax.lax.broadcasted_iota(jnp.int32, a3.shape, 1)
    mid = (lane >= GLA_GATE_RANK) & (lane < 2 * GLA_GATE_RANK)
    z = _dot(jnp.where(mid, a3 - a3_hi, a3_hi).astype(BF16), wg_ref[...]) + param_row(ROW_B_GATE, GLA_KEY_DIM)
    pu = proj(OFF_U, D_MODEL)
    log_f = (jnp.minimum(z, 0.0) - jnp.log(1.0 + jnp.exp(-jnp.abs(z)))) * (LOG2E / GLA_GATE_TEMP)

    row = jax.lax.broadcasted_iota(jnp.int32, (SG_CHUNK, SG_CHUNK), 0)
    col = jax.lax.broadcasted_iota(jnp.int32, (SG_CHUNK, SG_CHUNK), 1)
    mixed_g = []
    for g in range(SG_GROUPS):
        w_g = jnp.where(row >= col, ws_ref[g], 0.0).astype(BF16)
        gs = slice(g * LANES, (g + 1) * LANES)
        rhs = jnp.concatenate([v_sg[c * SG_CHUNK:(c + 1) * SG_CHUNK, gs] for c in range(MIX_BLOCK // SG_CHUNK)],
                              axis=1)
        mixed_g.append(_dot(w_g, rhs))
    mixed = jnp.concatenate(
        [jnp.concatenate([mg[:, c * LANES:(c + 1) * LANES] for mg in mixed_g], axis=1) + bs_ref[...]
         for c in range(MIX_BLOCK // SG_CHUNK)], axis=0)
    pga = proj(OFF_GA, D_MODEL)

    decays, prefixes, suffixes = [], [], []
    for c in range(n_chunks):
        f_hi, f_lo = _split_bf16(log_f[c * GLA_CHUNK:(c + 1) * GLA_CHUNK])
        expo = _dot(tab_ref[...], jnp.concatenate([f_hi, f_lo], axis=0))
        cum = expo[FINE_LEVELS * GLA_CHUNK:]
        level_expo = []
        for lev in range(GLA_LEVELS - FINE_LEVELS):
            w = GLA_CHUNK >> (lev + 1)
            mid = jnp.concatenate([jnp.broadcast_to(cum[b + w - 1:b + w], (2 * w, GLA_KEY_DIM))
                                   for b in range(0, GLA_CHUNK, 2 * w)], axis=0)
            level_expo.append(-jnp.abs(cum - mid))
        level_expo.append(expo[:FINE_LEVELS * GLA_CHUNK])
        decays.append(jnp.exp2(jnp.concatenate(level_expo, axis=0)).astype(BF16))
        prefixes.append(jnp.exp2(cum))
        suffixes.append(jnp.exp2(cum[GLA_CHUNK - 1:GLA_CHUNK] - cum))

    lvl2 = lvl_ref[...]
    pair_w = 2 * GLA_HEAD_K
    no_keys = jnp.zeros((GLA_HEAD_K, GLA_CHUNK), BF16)

    def pair_scores(lhs, keys_t):
        out = []
        for p in range(GLA_HEADS // 2):
            even = keys_t[(2 * p) * GLA_HEAD_K:(2 * p + 1) * GLA_HEAD_K]
            odd = keys_t[(2 * p + 1) * GLA_HEAD_K:(2 * p + 2) * GLA_HEAD_K]
            bd = jnp.concatenate([jnp.concatenate([even, no_keys], axis=1),
                                  jnp.concatenate([no_keys, odd], axis=1)], axis=0)
            out.append(_dot(lhs[:, p * pair_w:(p + 1) * pair_w], bd))
        return out

    def level_update(scores, c, lev):
        qc = q[c * GLA_CHUNK:(c + 1) * GLA_CHUNK]
        kc = k[c * GLA_CHUNK:(c + 1) * GLA_CHUNK]
        w = GLA_CHUNK >> (lev + 1)
        n_blk = GLA_CHUNK // w
        if w >= 8:
            mixed_qk = jnp.concatenate([(qc if (b % 2) else kc)[b * w:(b + 1) * w] for b in range(n_blk)], axis=0)
        else:
            mixed_qk = jnp.where(upper_ref[GLA_LEVELS - 1 - lev] != 0, qc, kc)
        zl = mixed_qk.astype(BF16) * decays[c][lev * GLA_CHUNK:(lev + 1) * GLA_CHUNK]
        zl_t = jnp.transpose(zl)
        if w >= BF16_ROWS:
            lhs = jnp.concatenate([zl[b * w:(b + 1) * w] for b in range(1, n_blk, 2)], axis=0)
            parts = pair_scores(lhs, zl_t)
            out = []
            for p in range(GLA_HEADS // 2):
                rows = []
                for b in range(n_blk):
                    old = scores[p][b * w:(b + 1) * w]
                    if b % 2:
                        new = parts[p][(b // 2) * w:(b // 2 + 1) * w]
                        old = jnp.where(lvl2[b * w:(b + 1) * w] == lev, new, old)
                    rows.append(old)
                out.append(jnp.concatenate(rows, axis=0))
            return out
        parts = pair_scores(zl, zl_t)
        return [jnp.where(lvl2 == lev, parts[p], scores[p]) for p in range(GLA_HEADS // 2)]

    def diagonal_update(scores, c):
        qk = q[c * GLA_CHUNK:(c + 1) * GLA_CHUNK] * k[c * GLA_CHUNK:(c + 1) * GLA_CHUNK]
        out = []
        for p in range(GLA_HEADS // 2):
            diag = [jnp.broadcast_to(jnp.sum(qk[:, h * GLA_HEAD_K:(h + 1) * GLA_HEAD_K], axis=1, keepdims=True),
                                     (GLA_CHUNK, GLA_CHUNK)) for h in (2 * p, 2 * p + 1)]
            out.append(jnp.where(lvl2 == GLA_LEVELS, jnp.concatenate(diag, axis=1), scores[p]))
        return out

    all_scores = [[jnp.zeros((GLA_CHUNK, 2 * GLA_CHUNK), F32) for _ in range(GLA_HEADS // 2)]
                  for _ in range(n_chunks)]
    m_acc = _sigmoid(pga) * (_gelu(pu) * mixed)
    for lev in range(GLA_LEVELS):
        all_scores = [level_update(all_scores[c], c, lev) for c in range(n_chunks)]
        if lev == 1:
            gv = proj(OFF_GV, GLA_VAL_DIM).astype(BF16)
        if lev == 4:
            r = proj(OFF_R, GLA_VAL_DIM)
    all_scores = [diagonal_update(all_scores[c], c) for c in range(n_chunks)]

    heads_sl = [slice(h * GLA_HEAD_K, (h + 1) * GLA_HEAD_K) for h in range(GLA_HEADS)]
    v_heads = [[gv[c * GLA_CHUNK:(c + 1) * GLA_CHUNK, h * GLA_HEAD_V:(h + 1) * GLA_HEAD_V]
                for h in range(GLA_HEADS)] for c in range(n_chunks)]
    kv, scale, q_ins = [], [], []
    for c in range(n_chunks):
        r0 = c * GLA_CHUNK
        q_ins.append((q[r0:r0 + GLA_CHUNK] * prefixes[c]).astype(BF16))
        k_out = (k[r0:r0 + GLA_CHUNK] * suffixes[c]).astype(BF16)
        total = prefixes[c][GLA_CHUNK - 1:GLA_CHUNK]
        kv.append([_dot_tn(k_out[:, sl], v_heads[c][h]) for h, sl in enumerate(heads_sl)])
        tcols = [jnp.transpose(jnp.broadcast_to(total[:, sl], (GLA_HEAD_K, GLA_HEAD_K))) for sl in heads_sl]
        scale.append([jnp.concatenate([t, t], axis=1) for t in tcols])
    states = [state_ref[h] for h in range(GLA_HEADS)]
    o_rows = []
    for c in range(n_chunks):
        heads = []
        for h, sl in enumerate(heads_sl):
            s_h = all_scores[c][h // 2][:, (h % 2) * GLA_CHUNK:(h % 2 + 1) * GLA_CHUNK].astype(BF16)
            o_h = _dot(jnp.concatenate([s_h, q_ins[c][:, sl]], axis=1),
                       jnp.concatenate([v_heads[c][h], states[h].astype(BF16)], axis=0))
            states[h] = states[h] * scale[c][h] + kv[c][h]
            o_h = o_h * jax.lax.rsqrt(jnp.mean(o_h * o_h, axis=-1, keepdims=True) + LN_EPS)
            heads.append(o_h)
        o_rows.append(jnp.concatenate(heads, axis=1))
    for h in range(GLA_HEADS):
        state_ref[h] = states[h]
    pgb = proj(OFF_GB, D_MODEL)
    y_b = (jnp.concatenate(o_rows, axis=0) * param_row(ROW_GN, GLA_VAL_DIM)) * (r * _sigmoid(r))
    m = (m_acc + _sigmoid(pgb) * y_b).astype(BF16)

    for r0, r1 in ((0, MIX_BLOCK - MIX_TAIL_ROWS), (MIX_BLOCK - MIX_TAIL_ROWS, MIX_BLOCK)):
        h1 = alpha * x_ref[0, r0:r1, :] + _dot(m[r0:r1], wout_ref[...])
        o_ref[0, r0:r1, :] = _layer_norm(h1, param_row(ROW_LN_G, D_MODEL), param_row(ROW_LN_B, D_MODEL))


def _ffn_kernel(h_ref, w1_ref, w2_ref, ln_g_ref, ln_b_ref, o_ref, *, alpha):
    h = h_ref[...]
    hb = h.astype(BF16)
    acc = alpha * h
    slab = FFN_SLAB

    def up(j):
        a = jnp.maximum(_dot(hb, w1_ref[:, j * slab:(j + 1) * slab]), 0.0)
        return (a * a).astype(BF16)

    n_slabs = D_FF // slab
    for j in range(n_slabs - 1):
        acc = acc + _dot(up(j), w2_ref[j * slab:(j + 1) * slab, :])
    a = up(n_slabs - 1)
    w2_last = w2_ref[(n_slabs - 1) * slab:, :]
    rows = FFN_BLOCK // FFN_TAIL_PIECES
    for r0 in range(0, FFN_BLOCK, rows):
        out = acc[r0:r0 + rows] + _dot(a[r0:r0 + rows], w2_last)
        o_ref[r0:r0 + rows, :] = _layer_norm(out, ln_g_ref[...], ln_b_ref[...])


def _resident(shape):
    nd = len(shape)
    return pl.BlockSpec(shape, lambda *_: (0,) * nd, pipeline_mode=pl.Buffered(1))


def _mixer_call(x, w1, w2, consts, alpha):
    bsz, t, d = x.shape
    blocks_per_row = t // MIX_BLOCK
    n_steps = bsz * blocks_per_row
    slab1, slab2 = w1.shape[0] // n_steps, w2.shape[0] // n_steps

    def slab(b, i):
        return (b * blocks_per_row + i, 0)

    return pl.pallas_call(
        functools.partial(_mixer_kernel, alpha=alpha),
        out_shape=(jax.ShapeDtypeStruct((bsz, t, d), F32),
                   jax.ShapeDtypeStruct(w1.shape, BF16), jax.ShapeDtypeStruct(w2.shape, BF16)),
        grid=(bsz, blocks_per_row),
        in_specs=[pl.BlockSpec((1, MIX_BLOCK, d), lambda b, i: (b, i, 0)),
                  pl.BlockSpec((slab1, w1.shape[1]), slab), pl.BlockSpec((slab2, w2.shape[1]), slab)]
        + [pl.BlockSpec(memory_space=pl.ANY)] + [_resident(c.shape) for c in consts[1:]],
        out_specs=(pl.BlockSpec((1, MIX_BLOCK, d), lambda b, i: (b, i, 0)),
                   pl.BlockSpec((slab1, w1.shape[1]), slab), pl.BlockSpec((slab2, w2.shape[1]), slab)),
        scratch_shapes=[pltpu.VMEM((GLA_HEADS, GLA_HEAD_K, GLA_HEAD_V), F32),
                        pltpu.VMEM((d, D_MAIN), BF16),
                        pltpu.VMEM((d, d), BF16),
                        pltpu.VMEM((2, W_STAGE_ROWS, d), F32),
                        pltpu.SemaphoreType.DMA((2,))],
        compiler_params=pltpu.CompilerParams(
            dimension_semantics=("arbitrary", "arbitrary"), vmem_limit_bytes=VMEM_LIMIT),
        name="mixer",
    )(x, w1, w2, *consts)


def _ffn_call(h, w1, w2, ln_g, ln_b, alpha):
    n, d = h.shape
    return pl.pallas_call(
        functools.partial(_ffn_kernel, alpha=alpha),
        out_shape=jax.ShapeDtypeStruct((n, d), F32),
        grid=(n // FFN_BLOCK,),
        in_specs=[pl.BlockSpec((FFN_BLOCK, d), lambda i: (i, 0)),
                  _resident(w1.shape), _resident(w2.shape), _resident(ln_g.shape), _resident(ln_b.shape)],
        out_specs=pl.BlockSpec((FFN_BLOCK, d), lambda i: (i, 0)),
        compiler_params=pltpu.CompilerParams(
            dimension_semantics=("arbitrary",), vmem_limit_bytes=VMEM_LIMIT),
        name="ffn",
    )(h, w1, w2, ln_g, ln_b)


def kernel(x, w_in, b_in, sg_ln_g, sg_ln_b, sg_w_s, sg_b_s, gla_w_gate2, gla_b_gate, gla_norm_g,
           w_out, ln1_g, ln1_b, w_ff1, w_ff2, ln2_g, ln2_b):
    bsz, t, d = x.shape
    depth = w_in.shape[0]
    alpha = (2.0 * depth) ** 0.25
    table_np, level_np = _decay_tables()
    table = jnp.asarray(np.concatenate([table_np, table_np], axis=1), BF16)
    level = jnp.asarray(np.concatenate([level_np, level_np], axis=1))
    upper = jnp.asarray(np.stack([np.broadcast_to(((np.arange(GLA_CHUNK) >> j) & 1)[:, None],
                                                  (GLA_CHUNK, GLA_KEY_DIM)) for j in range(FINE_LEVELS)]).astype(np.int32))
    low0, low1 = OFF_LOW, OFF_LOW + GLA_GATE_RANK
    rank = GLA_GATE_RANK
    h = x
    for l in range(depth):
        w_main = jnp.transpose(w_in[l])
        b_row = b_in[l:l + 1]
        rows = jnp.concatenate(
            [b_row[:, :low0], b_row[:, low1:],
             jnp.tile(b_row[:, low0:low1], (1, 3)), jnp.zeros((1, 2 * LANES - 3 * rank), F32),
             gla_b_gate[l:l + 1], sg_ln_g[l:l + 1], sg_ln_b[l:l + 1], gla_norm_g[l:l + 1],
             ln1_g[l:l + 1], ln1_b[l:l + 1]], axis=1)
        w_low = jnp.pad(jnp.tile(jnp.transpose(w_main[low0:low1]).astype(BF16), (1, 3)),
                        ((0, 0), (0, 2 * LANES - 3 * rank)))
        wg_hi = gla_w_gate2[l].astype(BF16)
        wg_lo = (gla_w_gate2[l] - wg_hi.astype(F32)).astype(BF16)
        wg = jnp.pad(jnp.concatenate([wg_hi, wg_hi, wg_lo], axis=0), ((0, LANES - 3 * rank), (0, 0)))
        bs_plane = jnp.repeat(jnp.transpose(sg_b_s[l]), D_MODEL // SG_GROUPS, axis=1)
        consts = (w_main, rows, w_low, wg, sg_w_s[l], bs_plane, table, level, upper, w_out[l])
        h, w1b, w2b = _mixer_call(h, w_ff1[l], w_ff2[l], consts, alpha)
        h = _ffn_call(h.reshape(bsz * t, d), w1b, w2b,
                      ln2_g[l][None, :], ln2_b[l][None, :], alpha).reshape(bsz, t, d)
    return h
```

```python
import functools
import math

import jax
import jax.numpy as jnp
import numpy as np
from jax.experimental import pallas as pl
from jax.experimental.pallas import tpu as pltpu

D_MODEL = 1024
SG_CHUNK = 128
SG_GROUPS = 8
GLA_HEADS = 4
GLA_KEY_DIM = 512
GLA_VAL_DIM = 1024
GLA_HEAD_K = GLA_KEY_DIM // GLA_HEADS
GLA_HEAD_V = GLA_VAL_DIM // GLA_HEADS
GLA_GATE_RANK = 16
GLA_GATE_TEMP = 16.0
D_FF = 4 * D_MODEL
LN_EPS = 1e-5

LANES = 128
BF16_ROWS = 16
GLA_CHUNK = 128
GLA_LEVELS = int(math.log2(GLA_CHUNK))
FINE_LEVELS = 3
MIX_BLOCK = 512
FFN_BLOCK = 1024
MIX_TAIL_ROWS = 256
FFN_TAIL_PIECES = 4
FFN_SLAB = 2048
W_STAGE_ROWS = 512
VMEM_LIMIT = 56 * 1024 * 1024

OFF_U, OFF_V, OFF_Q, OFF_K, OFF_GV, OFF_R, OFF_LOW, OFF_GA, OFF_GB = 0, 1024, 2048, 2560, 3072, 4096, 5120, 5136, 6160
D_MAIN = 7168

GELU_C = 0.7978845608028654
LOG2E = 1.4426950408889634

BF16 = jnp.bfloat16
F32 = jnp.float32


def _decay_tables():
    c = GLA_CHUNK
    blocks = []
    for lw in range(FINE_LEVELS - 1, -1, -1):
        a = np.zeros((c, c), np.float32)
        for t in range(c):
            if (t >> lw) & 1:
                a[t, (t >> lw) << lw:t + 1] = 1.0
            else:
                a[t, t + 1:((t >> lw) + 1) << lw] = 1.0
        blocks.append(a)
    blocks.append(np.tril(np.ones((c, c), np.float32)))
    table = np.concatenate(blocks, axis=0)
    t = np.arange(c)[:, None]
    s = np.arange(c)[None, :]
    x = t ^ s
    top = np.zeros_like(x)
    nz = x > 0
    top[nz] = np.floor(np.log2(x[nz])).astype(x.dtype)
    level = np.where(t > s, GLA_LEVELS - 1 - top, np.where(t == s, GLA_LEVELS, -1)).astype(np.int32)
    return table, level


def _layer_norm(x, g, b):
    mu = jnp.mean(x, axis=-1, keepdims=True)
    xc = x - mu
    var = jnp.mean(xc * xc, axis=-1, keepdims=True)
    return xc * jax.lax.rsqrt(var + LN_EPS) * g + b


def _gelu(x):
    hx = 0.5 * x
    return hx + hx * jnp.tanh(x * (GELU_C + (GELU_C * 0.044715) * (x * x)))


def _sigmoid(x):
    return 0.5 * jnp.tanh(0.5 * x) + 0.5


def _split_bf16(x):
    hi = x.astype(BF16)
    lo = (x - hi.astype(F32)).astype(BF16)
    return hi, lo


def _dot(a, b):
    return jnp.dot(a, b, preferred_element_type=F32)


def _dot_tn(a, b):
    return jax.lax.dot_general(a, b, (((0,), (0,)), ((), ())), preferred_element_type=F32)


def _cast_projection_weights(w_hbm, w_ref, stage_ref, sem):
    starts = [r for r in range(0, D_MAIN + GLA_GATE_RANK, W_STAGE_ROWS) if r < OFF_LOW]
    starts += list(range(OFF_LOW + GLA_GATE_RANK, D_MAIN + GLA_GATE_RANK, W_STAGE_ROWS))

    def copy(j):
        return pltpu.make_async_copy(w_hbm.at[pl.ds(starts[j], W_STAGE_ROWS)], stage_ref.at[j % 2], sem.at[j % 2])

    copy(0).start()
    for j, r0 in enumerate(starts):
        if j + 1 < len(starts):
            copy(j + 1).start()
        copy(j).wait()
        c0 = r0 if r0 < OFF_LOW else r0 - GLA_GATE_RANK
        w_ref[:, c0:c0 + W_STAGE_ROWS] = jnp.transpose(stage_ref[j % 2]).astype(BF16)


def _mixer_kernel(x_ref, w1f_ref, w2f_ref, w_hbm, b_ref, wlow_ref, blow_ref, wg_ref, bg_ref,
                  sg_g_ref, sg_b_ref, ws_ref, bs_ref, tab_ref, lvl_ref, upper_ref, gn_ref,
                  woutf_ref, ln_g_ref, ln_b_ref, o_ref, w1b_ref, w2b_ref,
                  state_ref, w_ref, wout_ref, stage_ref, sem, *, alpha):
    @pl.when((pl.program_id(0) == 0) & (pl.program_id(1) == 0))
    def _():
        _cast_projection_weights(w_hbm, w_ref, stage_ref, sem)
        wout_ref[...] = woutf_ref[...].astype(BF16)

    @pl.when(pl.program_id(1) == 0)
    def _():
        state_ref[...] = jnp.zeros_like(state_ref)

    w1b_ref[...] = w1f_ref[...].astype(BF16)
    w2b_ref[...] = w2f_ref[...].astype(BF16)

    x = x_ref[0]
    xh = x.astype(BF16)
    n_chunks = MIX_BLOCK // GLA_CHUNK

    def proj(off, width):
        c0 = off if off < OFF_LOW else off - GLA_GATE_RANK
        return _dot(xh, w_ref[:, c0:c0 + width]) + b_ref[:, c0:c0 + width]

    pv = proj(OFF_V, D_MODEL)
    q = proj(OFF_Q, GLA_KEY_DIM) * (GLA_HEAD_K ** -0.5)
    k = proj(OFF_K, GLA_KEY_DIM)
    a3 = (_dot(xh, wlow_ref[...]) + blow_ref[...])[:, :LANES]

    v_sg = _layer_norm(_gelu(pv), sg_g_ref[...], sg_b_ref[...]).astype(BF16)
    a3_hi = a3.astype(BF16).astype(F32)
    lane = jax.lax.broadcasted_iota(jnp.int32, a3.shape, 1)
    mid = (lane >= GLA_GATE_RANK) & (lane < 2 * GLA_GATE_RANK)
    z = _dot(jnp.where(mid, a3 - a3_hi, a3_hi).astype(BF16), wg_ref[...]) + bg_ref[...]
    pu = proj(OFF_U, D_MODEL)
    log_f = (jnp.minimum(z, 0.0) - jnp.log(1.0 + jnp.exp(-jnp.abs(z)))) * (LOG2E / GLA_GATE_TEMP)

    row = jax.lax.broadcasted_iota(jnp.int32, (SG_CHUNK, SG_CHUNK), 0)
    col = jax.lax.broadcasted_iota(jnp.int32, (SG_CHUNK, SG_CHUNK), 1)
    mixed_g = []
    for g in range(SG_GROUPS):
        w_g = jnp.where(row >= col, ws_ref[g], 0.0).astype(BF16)
        gs = slice(g * LANES, (g + 1) * LANES)
        rhs = jnp.concatenate([v_sg[c * SG_CHUNK:(c + 1) * SG_CHUNK, gs] for c in range(MIX_BLOCK // SG_CHUNK)],
                              axis=1)
        mixed_g.append(_dot(w_g, rhs))
    mixed = jnp.concatenate(
        [jnp.concatenate([mg[:, c * LANES:(c + 1) * LANES] for mg in mixed_g], axis=1) + bs_ref[...]
         for c in range(MIX_BLOCK // SG_CHUNK)], axis=0)
    pga = proj(OFF_GA, D_MODEL)

    decays, prefixes, suffixes = [], [], []
    for c in range(n_chunks):
        f_hi, f_lo = _split_bf16(log_f[c * GLA_CHUNK:(c + 1) * GLA_CHUNK])
        expo = _dot(tab_ref[...], jnp.concatenate([f_hi, f_lo], axis=0))
        cum = expo[FINE_LEVELS * GLA_CHUNK:]
        level_expo = []
        for lev in range(GLA_LEVELS - FINE_LEVELS):
            w = GLA_CHUNK >> (lev + 1)
            mid = jnp.concatenate([jnp.broadcast_to(cum[b + w - 1:b + w], (2 * w, GLA_KEY_DIM))
                                   for b in range(0, GLA_CHUNK, 2 * w)], axis=0)
            level_expo.append(-jnp.abs(cum - mid))
        level_expo.append(expo[:FINE_LEVELS * GLA_CHUNK])
        decays.append(jnp.exp2(jnp.concatenate(level_expo, axis=0)).astype(BF16))
        prefixes.append(jnp.exp2(cum))
        suffixes.append(jnp.exp2(cum[GLA_CHUNK - 1:GLA_CHUNK] - cum))

    pair_w = 2 * GLA_HEAD_K
    no_keys = jnp.zeros((GLA_HEAD_K, GLA_CHUNK), BF16)

    def pair_scores(lhs, keys_t):
        out = []
        for p in range(GLA_HEADS // 2):
            even = keys_t[(2 * p) * GLA_HEAD_K:(2 * p + 1) * GLA_HEAD_K]
            odd = keys_t[(2 * p + 1) * GLA_HEAD_K:(2 * p + 2) * GLA_HEAD_K]
            bd = jnp.concatenate([jnp.concatenate([even, no_keys], axis=1),
                                  jnp.concatenate([no_keys, odd], axis=1)], axis=0)
            out.append(_dot(lhs[:, p * pair_w:(p + 1) * pair_w], bd))
        return out

    def level_update(scores, c, lev):
        qc = q[c * GLA_CHUNK:(c + 1) * GLA_CHUNK]
        kc = k[c * GLA_CHUNK:(c + 1) * GLA_CHUNK]
        w = GLA_CHUNK >> (lev + 1)
        n_blk = GLA_CHUNK // w
        if w >= 8:
            mixed_qk = jnp.concatenate([(qc if (b % 2) else kc)[b * w:(b + 1) * w] for b in range(n_blk)], axis=0)
        else:
            mixed_qk = jnp.where(upper_ref[GLA_LEVELS - 1 - lev] != 0, qc, kc)
        zl = mixed_qk.astype(BF16) * decays[c][lev * GLA_CHUNK:(lev + 1) * GLA_CHUNK]
        zl_t = jnp.transpose(zl)
        if w >= BF16_ROWS:
            lhs = jnp.concatenate([zl[b * w:(b + 1) * w] for b in range(1, n_blk, 2)], axis=0)
            parts = pair_scores(lhs, zl_t)
            out = []
            for p in range(GLA_HEADS // 2):
                rows = []
                for b in range(n_blk):
                    old = scores[p][b * w:(b + 1) * w]
                    if b % 2:
                        new = parts[p][(b // 2) * w:(b // 2 + 1) * w]
                        old = jnp.where(lvl_ref[b * w:(b + 1) * w, :] == lev, new, old)
                    rows.append(old)
                out.append(jnp.concatenate(rows, axis=0))
            return out
        parts = pair_scores(zl, zl_t)
        return [jnp.where(lvl_ref[...] == lev, parts[p], scores[p]) for p in range(GLA_HEADS // 2)]

    def diagonal_update(scores, c):
        qk = q[c * GLA_CHUNK:(c + 1) * GLA_CHUNK] * k[c * GLA_CHUNK:(c + 1) * GLA_CHUNK]
        out = []
        for p in range(GLA_HEADS // 2):
            diag = [jnp.broadcast_to(jnp.sum(qk[:, h * GLA_HEAD_K:(h + 1) * GLA_HEAD_K], axis=1, keepdims=True),
                                     (GLA_CHUNK, GLA_CHUNK)) for h in (2 * p, 2 * p + 1)]
            out.append(jnp.where(lvl_ref[...] == GLA_LEVELS, jnp.concatenate(diag, axis=1), scores[p]))
        return out

    all_scores = [[jnp.zeros((GLA_CHUNK, 2 * GLA_CHUNK), F32) for _ in range(GLA_HEADS // 2)]
                  for _ in range(n_chunks)]
    m_acc = _sigmoid(pga) * (_gelu(pu) * mixed)
    for lev in range(GLA_LEVELS):
        all_scores = [level_update(all_scores[c], c, lev) for c in range(n_chunks)]
        if lev == 1:
            gv = proj(OFF_GV, GLA_VAL_DIM).astype(BF16)
        if lev == 4:
            r = proj(OFF_R, GLA_VAL_DIM)
    all_scores = [diagonal_update(all_scores[c], c) for c in range(n_chunks)]

    heads_sl = [slice(h * GLA_HEAD_K, (h + 1) * GLA_HEAD_K) for h in range(GLA_HEADS)]
    v_heads = [[gv[c * GLA_CHUNK:(c + 1) * GLA_CHUNK, h * GLA_HEAD_V:(h + 1) * GLA_HEAD_V]
                for h in range(GLA_HEADS)] for c in range(n_chunks)]
    kv, scale, q_ins = [], [], []
    for c in range(n_chunks):
        r0 = c * GLA_CHUNK
        q_ins.append((q[r0:r0 + GLA_CHUNK] * prefixes[c]).astype(BF16))
        k_out = (k[r0:r0 + GLA_CHUNK] * suffixes[c]).astype(BF16)
        total = prefixes[c][GLA_CHUNK - 1:GLA_CHUNK]
        kv.append([_dot_tn(k_out[:, sl], v_heads[c][h]) for h, sl in enumerate(heads_sl)])
        tcols = [jnp.transpose(jnp.broadcast_to(total[:, sl], (GLA_HEAD_K, GLA_HEAD_K))) for sl in heads_sl]
        scale.append([jnp.concatenate([t, t], axis=1) for t in tcols])
    states = [state_ref[h] for h in range(GLA_HEADS)]
    o_rows = []
    for c in range(n_chunks):
        heads = []
        for h, sl in enumerate(heads_sl):
            s_h = all_scores[c][h // 2][:, (h % 2) * GLA_CHUNK:(h % 2 + 1) * GLA_CHUNK].astype(BF16)
            o_h = _dot(jnp.concatenate([s_h, q_ins[c][:, sl]], axis=1),
                       jnp.concatenate([v_heads[c][h], states[h].astype(BF16)], axis=0))
            states[h] = states[h] * scale[c][h] + kv[c][h]
            o_h = o_h * jax.lax.rsqrt(jnp.mean(o_h * o_h, axis=-1, keepdims=True) + LN_EPS)
            heads.append(o_h)
        o_rows.append(jnp.concatenate(heads, axis=1))
    for h in range(GLA_HEADS):
        state_ref[h] = states[h]
    pgb = proj(OFF_GB, D_MODEL)
    y_b = (jnp.concatenate(o_rows, axis=0) * gn_ref[...]) * (r * _sigmoid(r))
    m = (m_acc + _sigmoid(pgb) * y_b).astype(BF16)

    for r0, r1 in ((0, MIX_BLOCK - MIX_TAIL_ROWS), (MIX_BLOCK - MIX_TAIL_ROWS, MIX_BLOCK)):
        h1 = alpha * x_ref[0, r0:r1, :] + _dot(m[r0:r1], wout_ref[...])
        o_ref[0, r0:r1, :] = _layer_norm(h1, ln_g_ref[...], ln_b_ref[...])


def _ffn_kernel(h_ref, w1_ref, w2_ref, ln_g_ref, ln_b_ref, o_ref, *, alpha):
    h = h_ref[...]
    hb = h.astype(BF16)
    acc = alpha * h
    slab = FFN_SLAB

    def up(j):
        a = jnp.maximum(_dot(hb, w1_ref[:, j * slab:(j + 1) * slab]), 0.0)
        return (a * a).astype(BF16)

    n_slabs = D_FF // slab
    for j in range(n_slabs - 1):
        acc = acc + _dot(up(j), w2_ref[j * slab:(j + 1) * slab, :])
    a = up(n_slabs - 1)
    w2_last = w2_ref[(n_slabs - 1) * slab:, :]
    rows = FFN_BLOCK // FFN_TAIL_PIECES
    for r0 in range(0, FFN_BLOCK, rows):
        out = acc[r0:r0 + rows] + _dot(a[r0:r0 + rows], w2_last)
        o_ref[r0:r0 + rows, :] = _layer_norm(out, ln_g_ref[...], ln_b_ref[...])


def _resident(shape):
    nd = len(shape)
    return pl.BlockSpec(shape, lambda *_: (0,) * nd, pipeline_mode=pl.Buffered(1))


def _mixer_call(x, w1, w2, consts, alpha):
    bsz, t, d = x.shape
    blocks_per_row = t // MIX_BLOCK
    n_steps = bsz * blocks_per_row
    slab1, slab2 = w1.shape[0] // n_steps, w2.shape[0] // n_steps

    def slab(b, i):
        return (b * blocks_per_row + i, 0)

    return pl.pallas_call(
        functools.partial(_mixer_kernel, alpha=alpha),
        out_shape=(jax.ShapeDtypeStruct((bsz, t, d), F32),
                   jax.ShapeDtypeStruct(w1.shape, BF16), jax.ShapeDtypeStruct(w2.shape, BF16)),
        grid=(bsz, blocks_per_row),
        in_specs=[pl.BlockSpec((1, MIX_BLOCK, d), lambda b, i: (b, i, 0)),
                  pl.BlockSpec((slab1, w1.shape[1]), slab), pl.BlockSpec((slab2, w2.shape[1]), slab)]
        + [pl.BlockSpec(memory_space=pl.ANY)] + [_resident(c.shape) for c in consts[1:]],
        out_specs=(pl.BlockSpec((1, MIX_BLOCK, d), lambda b, i: (b, i, 0)),
                   pl.BlockSpec((slab1, w1.shape[1]), slab), pl.BlockSpec((slab2, w2.shape[1]), slab)),
        scratch_shapes=[pltpu.VMEM((GLA_HEADS, GLA_HEAD_K, GLA_HEAD_V), F32),
                        pltpu.VMEM((d, D_MAIN), BF16),
                        pltpu.VMEM((d, d), BF16),
                        pltpu.VMEM((2, W_STAGE_ROWS, d), F32),
                        pltpu.SemaphoreType.DMA((2,))],
        compiler_params=pltpu.CompilerParams(
            dimension_semantics=("arbitrary", "arbitrary"), vmem_limit_bytes=VMEM_LIMIT),
        name="mixer",
    )(x, w1, w2, *consts)


def _ffn_call(h, w1, w2, ln_g, ln_b, alpha):
    n, d = h.shape
    return pl.pallas_call(
        functools.partial(_ffn_kernel, alpha=alpha),
        out_shape=jax.ShapeDtypeStruct((n, d), F32),
        grid=(n // FFN_BLOCK,),
        in_specs=[pl.BlockSpec((FFN_BLOCK, d), lambda i: (i, 0)),
                  _resident(w1.shape), _resident(w2.shape), _resident(ln_g.shape), _resident(ln_b.shape)],
        out_specs=pl.BlockSpec((FFN_BLOCK, d), lambda i: (i, 0)),
        compiler_params=pltpu.CompilerParams(
            dimension_semantics=("arbitrary",), vmem_limit_bytes=VMEM_LIMIT),
        name="ffn",
    )(h, w1, w2, ln_g, ln_b)


def kernel(x, w_in, b_in, sg_ln_g, sg_ln_b, sg_w_s, sg_b_s, gla_w_gate2, gla_b_gate, gla_norm_g,
           w_out, ln1_g, ln1_b, w_ff1, w_ff2, ln2_g, ln2_b):
    bsz, t, d = x.shape
    depth = w_in.shape[0]
    alpha = (2.0 * depth) ** 0.25
    table_np, level_np = _decay_tables()
    table = jnp.asarray(np.concatenate([table_np, table_np], axis=1), BF16)
    level = jnp.asarray(np.concatenate([level_np, level_np], axis=1))
    upper = jnp.asarray(np.stack([np.broadcast_to(((np.arange(GLA_CHUNK) >> j) & 1)[:, None],
                                                  (GLA_CHUNK, GLA_KEY_DIM)) for j in range(FINE_LEVELS)]).astype(np.int32))
    low0, low1 = OFF_LOW, OFF_LOW + GLA_GATE_RANK
    rank = GLA_GATE_RANK
    h = x
    for l in range(depth):
        w_main = jnp.transpose(w_in[l])
        b_main = jnp.concatenate([b_in[l][:low0], b_in[l][low1:]])[None, :]
        w_low = jnp.pad(jnp.tile(jnp.transpose(w_main[low0:low1]).astype(BF16), (1, 3)),
                        ((0, 0), (0, 2 * LANES - 3 * rank)))
        b_low = jnp.pad(jnp.tile(b_in[l][low0:low1], 3), (0, 2 * LANES - 3 * rank))[None, :]
        wg_hi = gla_w_gate2[l].astype(BF16)
        wg_lo = (gla_w_gate2[l] - wg_hi.astype(F32)).astype(BF16)
        wg = jnp.pad(jnp.concatenate([wg_hi, wg_hi, wg_lo], axis=0), ((0, LANES - 3 * rank), (0, 0)))
        bs_plane = jnp.repeat(jnp.transpose(sg_b_s[l]), D_MODEL // SG_GROUPS, axis=1)
        consts = (w_main, b_main, w_low, b_low, wg, gla_b_gate[l][None, :],
                  sg_ln_g[l][None, :], sg_ln_b[l][None, :], sg_w_s[l], bs_plane, table, level, upper,
                  gla_norm_g[l][None, :], w_out[l], ln1_g[l][None, :], ln1_b[l][None, :])
        h, w1b, w2b = _mixer_call(h, w_ff1[l], w_ff2[l], consts, alpha)
        h = _ffn_call(h.reshape(bsz * t, d), w1b, w2b,
                      ln2_g[l][None, :], ln2_b[l][None, :], alpha).reshape(bsz, t, d)
    return h
```

```python
import functools
import math

import jax
import jax.numpy as jnp
import numpy as np
from jax.experimental import pallas as pl
from jax.experimental.pallas import tpu as pltpu

D_MODEL = 1024
SG_CHUNK = 128
SG_GROUPS = 8
GLA_HEADS = 4
GLA_KEY_DIM = 512
GLA_VAL_DIM = 1024
GLA_HEAD_K = GLA_KEY_DIM // GLA_HEADS
GLA_HEAD_V = GLA_VAL_DIM // GLA_HEADS
GLA_GATE_RANK = 16
GLA_GATE_TEMP = 16.0
D_FF = 4 * D_MODEL
LN_EPS = 1e-5

LANES = 128
BF16_ROWS = 16
GLA_CHUNK = 128
GLA_LEVELS = int(math.log2(GLA_CHUNK))
FINE_LEVELS = 3
MIX_BLOCK = 512
FFN_BLOCK = 1024
MIX_TAIL_ROWS = 256
FFN_TAIL_PIECES = 4
FFN_SLAB = 2048
W_STAGE_ROWS = 512
VMEM_LIMIT = 56 * 1024 * 1024

OFF_U, OFF_V, OFF_Q, OFF_K, OFF_GV, OFF_R, OFF_LOW, OFF_GA, OFF_GB = 0, 1024, 2048, 2560, 3072, 4096, 5120, 5136, 6160
D_MAIN = 7168

GELU_C = 0.7978845608028654
LOG2E = 1.4426950408889634

BF16 = jnp.bfloat16
F32 = jnp.float32


def _decay_tables():
    c = GLA_CHUNK
    blocks = []
    for lw in range(FINE_LEVELS - 1, -1, -1):
        a = np.zeros((c, c), np.float32)
        for t in range(c):
            if (t >> lw) & 1:
                a[t, (t >> lw) << lw:t + 1] = 1.0
            else:
                a[t, t + 1:((t >> lw) + 1) << lw] = 1.0
        blocks.append(a)
    blocks.append(np.tril(np.ones((c, c), np.float32)))
    table = np.concatenate(blocks, axis=0)
    t = np.arange(c)[:, None]
    s = np.arange(c)[None, :]
    x = t ^ s
    top = np.zeros_like(x)
    nz = x > 0
    top[nz] = np.floor(np.log2(x[nz])).astype(x.dtype)
    level = np.where(t > s, GLA_LEVELS - 1 - top, np.where(t == s, GLA_LEVELS, -1)).astype(np.int32)
    return table, level


def _layer_norm(x, g, b):
    mu = jnp.mean(x, axis=-1, keepdims=True)
    xc = x - mu
    var = jnp.mean(xc * xc, axis=-1, keepdims=True)
    return xc * jax.lax.rsqrt(var + LN_EPS) * g + b


def _gelu(x):
    hx = 0.5 * x
    return hx + hx * jnp.tanh(x * (GELU_C + (GELU_C * 0.044715) * (x * x)))


def _sigmoid(x):
    return 0.5 * jnp.tanh(0.5 * x) + 0.5


def _split_bf16(x):
    hi = x.astype(BF16)
    lo = (x - hi.astype(F32)).astype(BF16)
    return hi, lo


def _dot(a, b):
    return jnp.dot(a, b, preferred_element_type=F32)


def _dot_tn(a, b):
    return jax.lax.dot_general(a, b, (((0,), (0,)), ((), ())), preferred_element_type=F32)


def _cast_projection_weights(w_hbm, w_ref, stage_ref, sem):
    starts = [r for r in range(0, D_MAIN + GLA_GATE_RANK, W_STAGE_ROWS) if r < OFF_LOW]
    starts += list(range(OFF_LOW + GLA_GATE_RANK, D_MAIN + GLA_GATE_RANK, W_STAGE_ROWS))

    def copy(j):
        return pltpu.make_async_copy(w_hbm.at[pl.ds(starts[j], W_STAGE_ROWS)], stage_ref.at[j % 2], sem.at[j % 2])

    copy(0).start()
    for j, r0 in enumerate(starts):
        if j + 1 < len(starts):
            copy(j + 1).start()
        copy(j).wait()
        c0 = r0 if r0 < OFF_LOW else r0 - GLA_GATE_RANK
        w_ref[:, c0:c0 + W_STAGE_ROWS] = jnp.transpose(stage_ref[j % 2]).astype(BF16)


def _mixer_kernel(x_ref, w1f_ref, w2f_ref, w_hbm, b_ref, wlow_ref, blow_ref, wg_ref, bg_ref,
                  sg_g_ref, sg_b_ref, ws_ref, bs_ref, tab_ref, lvl_ref, upper_ref, gn_ref,
                  woutf_ref, ln_g_ref, ln_b_ref, o_ref, w1b_ref, w2b_ref,
                  state_ref, w_ref, wout_ref, stage_ref, sem, *, alpha):
    @pl.when((pl.program_id(0) == 0) & (pl.program_id(1) == 0))
    def _():
        _cast_projection_weights(w_hbm, w_ref, stage_ref, sem)
        wout_ref[...] = woutf_ref[...].astype(BF16)

    @pl.when(pl.program_id(1) == 0)
    def _():
        state_ref[...] = jnp.zeros_like(state_ref)

    w1b_ref[...] = w1f_ref[...].astype(BF16)
    w2b_ref[...] = w2f_ref[...].astype(BF16)

    x = x_ref[0]
    xh = x.astype(BF16)
    n_chunks = MIX_BLOCK // GLA_CHUNK

    def proj(off, width):
        c0 = off if off < OFF_LOW else off - GLA_GATE_RANK
        return _dot(xh, w_ref[:, c0:c0 + width]) + b_ref[:, c0:c0 + width]

    pv = proj(OFF_V, D_MODEL)
    q = proj(OFF_Q, GLA_KEY_DIM) * (GLA_HEAD_K ** -0.5)
    k = proj(OFF_K, GLA_KEY_DIM)
    a3 = (_dot(xh, wlow_ref[...]) + blow_ref[...])[:, :LANES]

    v_sg = jnp.concatenate(
        [_layer_norm(_gelu(pv[c * SG_CHUNK:(c + 1) * SG_CHUNK]), sg_g_ref[...], sg_b_ref[...]).astype(BF16)
         for c in range(MIX_BLOCK // SG_CHUNK)], axis=0)
    a3_hi = a3.astype(BF16).astype(F32)
    lane = jax.lax.broadcasted_iota(jnp.int32, a3.shape, 1)
    mid = (lane >= GLA_GATE_RANK) & (lane < 2 * GLA_GATE_RANK)
    z = _dot(jnp.where(mid, a3 - a3_hi, a3_hi).astype(BF16), wg_ref[...]) + bg_ref[...]
    pu = proj(OFF_U, D_MODEL)
    log_f = (jnp.minimum(z, 0.0) - jnp.log(1.0 + jnp.exp(-jnp.abs(z)))) * (LOG2E / GLA_GATE_TEMP)

    row = jax.lax.broadcasted_iota(jnp.int32, (SG_CHUNK, SG_CHUNK), 0)
    col = jax.lax.broadcasted_iota(jnp.int32, (SG_CHUNK, SG_CHUNK), 1)
    mixed_g = []
    for g in range(SG_GROUPS):
        w_g = jnp.where(row >= col, ws_ref[g], 0.0).astype(BF16)
        gs = slice(g * LANES, (g + 1) * LANES)
        rhs = jnp.concatenate([v_sg[c * SG_CHUNK:(c + 1) * SG_CHUNK, gs] for c in range(MIX_BLOCK // SG_CHUNK)],
                              axis=1)
        mixed_g.append(_dot(w_g, rhs))
    mixed = jnp.concatenate(
        [jnp.concatenate([mg[:, c * LANES:(c + 1) * LANES] for mg in mixed_g], axis=1) + bs_ref[...]
         for c in range(MIX_BLOCK // SG_CHUNK)], axis=0)
    pga = proj(OFF_GA, D_MODEL)

    decays, prefixes, suffixes = [], [], []
    for c in range(n_chunks):
        f_hi, f_lo = _split_bf16(log_f[c * GLA_CHUNK:(c + 1) * GLA_CHUNK])
        expo = _dot(tab_ref[...], jnp.concatenate([f_hi, f_lo], axis=0))
        cum = expo[FINE_LEVELS * GLA_CHUNK:]
        level_expo = []
        for lev in range(GLA_LEVELS - FINE_LEVELS):
            w = GLA_CHUNK >> (lev + 1)
            mid = jnp.concatenate([jnp.broadcast_to(cum[b + w - 1:b + w], (2 * w, GLA_KEY_DIM))
                                   for b in range(0, GLA_CHUNK, 2 * w)], axis=0)
            level_expo.append(-jnp.abs(cum - mid))
        level_expo.append(expo[:FINE_LEVELS * GLA_CHUNK])
        decays.append(jnp.exp2(jnp.concatenate(level_expo, axis=0)).astype(BF16))
        prefixes.append(jnp.exp2(cum))
        suffixes.append(jnp.exp2(cum[GLA_CHUNK - 1:GLA_CHUNK] - cum))

    pair_w = 2 * GLA_HEAD_K
    no_keys = jnp.zeros((GLA_HEAD_K, GLA_CHUNK), BF16)

    def pair_scores(lhs, keys_t):
        out = []
        for p in range(GLA_HEADS // 2):
            even = keys_t[(2 * p) * GLA_HEAD_K:(2 * p + 1) * GLA_HEAD_K]
            odd = keys_t[(2 * p + 1) * GLA_HEAD_K:(2 * p + 2) * GLA_HEAD_K]
            bd = jnp.concatenate([jnp.concatenate([even, no_keys], axis=1),
                                  jnp.concatenate([no_keys, odd], axis=1)], axis=0)
            out.append(_dot(lhs[:, p * pair_w:(p + 1) * pair_w], bd))
        return out

    def level_update(scores, c, lev):
        qc = q[c * GLA_CHUNK:(c + 1) * GLA_CHUNK]
        kc = k[c * GLA_CHUNK:(c + 1) * GLA_CHUNK]
        w = GLA_CHUNK >> (lev + 1)
        n_blk = GLA_CHUNK // w
        if w >= 8:
            mixed_qk = jnp.concatenate([(qc if (b % 2) else kc)[b * w:(b + 1) * w] for b in range(n_blk)], axis=0)
        else:
            mixed_qk = jnp.where(upper_ref[GLA_LEVELS - 1 - lev] != 0, qc, kc)
        zl = mixed_qk.astype(BF16) * decays[c][lev * GLA_CHUNK:(lev + 1) * GLA_CHUNK]
        zl_t = jnp.transpose(zl)
        if w >= BF16_ROWS:
            lhs = jnp.concatenate([zl[b * w:(b + 1) * w] for b in range(1, n_blk, 2)], axis=0)
            parts = pair_scores(lhs, zl_t)
            out = []
            for p in range(GLA_HEADS // 2):
                rows = []
                for b in range(n_blk):
                    old = scores[p][b * w:(b + 1) * w]
                    if b % 2:
                        new = parts[p][(b // 2) * w:(b // 2 + 1) * w]
                        old = jnp.where(lvl_ref[b * w:(b + 1) * w, :] == lev, new, old)
                    rows.append(old)
                out.append(jnp.concatenate(rows, axis=0))
            return out
        parts = pair_scores(zl, zl_t)
        return [jnp.where(lvl_ref[...] == lev, parts[p], scores[p]) for p in range(GLA_HEADS // 2)]

    def diagonal_update(scores, c):
        qk = q[c * GLA_CHUNK:(c + 1) * GLA_CHUNK] * k[c * GLA_CHUNK:(c + 1) * GLA_CHUNK]
        out = []
        for p in range(GLA_HEADS // 2):
            diag = [jnp.broadcast_to(jnp.sum(qk[:, h * GLA_HEAD_K:(h + 1) * GLA_HEAD_K], axis=1, keepdims=True),
                                     (GLA_CHUNK, GLA_CHUNK)) for h in (2 * p, 2 * p + 1)]
            out.append(jnp.where(lvl_ref[...] == GLA_LEVELS, jnp.concatenate(diag, axis=1), scores[p]))
        return out

    all_scores = [[jnp.zeros((GLA_CHUNK, 2 * GLA_CHUNK), F32) for _ in range(GLA_HEADS // 2)]
                  for _ in range(n_chunks)]
    m_acc = jnp.concatenate(
        [_sigmoid(pga[c * SG_CHUNK:(c + 1) * SG_CHUNK])
         * (_gelu(pu[c * SG_CHUNK:(c + 1) * SG_CHUNK]) * mixed[c * SG_CHUNK:(c + 1) * SG_CHUNK])
         for c in range(MIX_BLOCK // SG_CHUNK)], axis=0)
    for lev in range(GLA_LEVELS):
        all_scores = [level_update(all_scores[c], c, lev) for c in range(n_chunks)]
        if lev == 1:
            gv = proj(OFF_GV, GLA_VAL_DIM).astype(BF16)
        if lev == 4:
            r = proj(OFF_R, GLA_VAL_DIM)
    all_scores = [diagonal_update(all_scores[c], c) for c in range(n_chunks)]

    heads_sl = [slice(h * GLA_HEAD_K, (h + 1) * GLA_HEAD_K) for h in range(GLA_HEADS)]
    v_heads = [[gv[c * GLA_CHUNK:(c + 1) * GLA_CHUNK, h * GLA_HEAD_V:(h + 1) * GLA_HEAD_V]
                for h in range(GLA_HEADS)] for c in range(n_chunks)]
    kv, scale, q_ins = [], [], []
    for c in range(n_chunks):
        r0 = c * GLA_CHUNK
        q_ins.append((q[r0:r0 + GLA_CHUNK] * prefixes[c]).astype(BF16))
        k_out = (k[r0:r0 + GLA_CHUNK] * suffixes[c]).astype(BF16)
        total = prefixes[c][GLA_CHUNK - 1:GLA_CHUNK]
        kv.append([_dot_tn(k_out[:, sl], v_heads[c][h]) for h, sl in enumerate(heads_sl)])
        tcols = [jnp.transpose(jnp.broadcast_to(total[:, sl], (GLA_HEAD_K, GLA_HEAD_K))) for sl in heads_sl]
        scale.append([jnp.concatenate([t, t], axis=1) for t in tcols])
    states = [state_ref[h] for h in range(GLA_HEADS)]
    o_rows = []
    for c in range(n_chunks):
        heads = []
        for h, sl in enumerate(heads_sl):
            s_h = all_scores[c][h // 2][:, (h % 2) * GLA_CHUNK:(h % 2 + 1) * GLA_CHUNK].astype(BF16)
            o_h = _dot(jnp.concatenate([s_h, q_ins[c][:, sl]], axis=1),
                       jnp.concatenate([v_heads[c][h], states[h].astype(BF16)], axis=0))
            states[h] = states[h] * scale[c][h] + kv[c][h]
            o_h = o_h * jax.lax.rsqrt(jnp.mean(o_h * o_h, axis=-1, keepdims=True) + LN_EPS)
            heads.append(o_h)
        o_rows.append(jnp.concatenate(heads, axis=1))
    for h in range(GLA_HEADS):
        state_ref[h] = states[h]
    pgb = proj(OFF_GB, D_MODEL)
    y_b = (jnp.concatenate(o_rows, axis=0) * gn_ref[...]) * (r * _sigmoid(r))
    m = (m_acc + _sigmoid(pgb) * y_b).astype(BF16)

    for r0, r1 in ((0, MIX_BLOCK - MIX_TAIL_ROWS), (MIX_BLOCK - MIX_TAIL_ROWS, MIX_BLOCK)):
        h1 = alpha * x_ref[0, r0:r1, :] + _dot(m[r0:r1], wout_ref[...])
        o_ref[0, r0:r1, :] = _layer_norm(h1, ln_g_ref[...], ln_b_ref[...])


def _ffn_kernel(h_ref, w1_ref, w2_ref, ln_g_ref, ln_b_ref, o_ref, *, alpha):
    h = h_ref[...]
    hb = h.astype(BF16)
    acc = alpha * h
    slab = FFN_SLAB

    def up(j):
        a = jnp.maximum(_dot(hb, w1_ref[:, j * slab:(j + 1) * slab]), 0.0)
        return (a * a).astype(BF16)

    n_slabs = D_FF // slab
    for j in range(n_slabs - 1):
        acc = acc + _dot(up(j), w2_ref[j * slab:(j + 1) * slab, :])
    a = up(n_slabs - 1)
    w2_last = w2_ref[(n_slabs - 1) * slab:, :]
    rows = FFN_BLOCK // FFN_TAIL_PIECES
    for r0 in range(0, FFN_BLOCK, rows):
        out = acc[r0:r0 + rows] + _dot(a[r0:r0 + rows], w2_last)
        o_ref[r0:r0 + rows, :] = _layer_norm(out, ln_g_ref[...], ln_b_ref[...])


def _resident(shape):
    nd = len(shape)
    return pl.BlockSpec(shape, lambda *_: (0,) * nd, pipeline_mode=pl.Buffered(1))


def _mixer_call(x, w1, w2, consts, alpha):
    bsz, t, d = x.shape
    blocks_per_row = t // MIX_BLOCK
    n_steps = bsz * blocks_per_row
    slab1, slab2 = w1.shape[0] // n_steps, w2.shape[0] // n_steps

    def slab(b, i):
        return (b * blocks_per_row + i, 0)

    return pl.pallas_call(
        functools.partial(_mixer_kernel, alpha=alpha),
        out_shape=(jax.ShapeDtypeStruct((bsz, t, d), F32),
                   jax.ShapeDtypeStruct(w1.shape, BF16), jax.ShapeDtypeStruct(w2.shape, BF16)),
        grid=(bsz, blocks_per_row),
        in_specs=[pl.BlockSpec((1, MIX_BLOCK, d), lambda b, i: (b, i, 0)),
                  pl.BlockSpec((slab1, w1.shape[1]), slab), pl.BlockSpec((slab2, w2.shape[1]), slab)]
        + [pl.BlockSpec(memory_space=pl.ANY)] + [_resident(c.shape) for c in consts[1:]],
        out_specs=(pl.BlockSpec((1, MIX_BLOCK, d), lambda b, i: (b, i, 0)),
                   pl.BlockSpec((slab1, w1.shape[1]), slab), pl.BlockSpec((slab2, w2.shape[1]), slab)),
        scratch_shapes=[pltpu.VMEM((GLA_HEADS, GLA_HEAD_K, GLA_HEAD_V), F32),
                        pltpu.VMEM((d, D_MAIN), BF16),
                        pltpu.VMEM((d, d), BF16),
                        pltpu.VMEM((2, W_STAGE_ROWS, d), F32),
                        pltpu.SemaphoreType.DMA((2,))],
        compiler_params=pltpu.CompilerParams(
            dimension_semantics=("arbitrary", "arbitrary"), vmem_limit_bytes=VMEM_LIMIT),
        name="mixer",
    )(x, w1, w2, *consts)


def _ffn_call(h, w1, w2, ln_g, ln_b, alpha):
    n, d = h.shape
    return pl.pallas_call(
        functools.partial(_ffn_kernel, alpha=alpha),
        out_shape=jax.ShapeDtypeStruct((n, d), F32),
        grid=(n // FFN_BLOCK,),
        in_specs=[pl.BlockSpec((FFN_BLOCK, d), lambda i: (i, 0)),
                  _resident(w1.shape), _resident(w2.shape), _resident(ln_g.shape), _resident(ln_b.shape)],
        out_specs=pl.BlockSpec((FFN_BLOCK, d), lambda i: (i, 0)),
        compiler_params=pltpu.CompilerParams(
            dimension_semantics=("arbitrary",), vmem_limit_bytes=VMEM_LIMIT),
        name="ffn",
    )(h, w1, w2, ln_g, ln_b)


def kernel(x, w_in, b_in, sg_ln_g, sg_ln_b, sg_w_s, sg_b_s, gla_w_gate2, gla_b_gate, gla_norm_g,
           w_out, ln1_g, ln1_b, w_ff1, w_ff2, ln2_g, ln2_b):
    bsz, t, d = x.shape
    depth = w_in.shape[0]
    alpha = (2.0 * depth) ** 0.25
    table_np, level_np = _decay_tables()
    table = jnp.asarray(np.concatenate([table_np, table_np], axis=1), BF16)
    level = jnp.asarray(np.concatenate([level_np, level_np], axis=1))
    upper = jnp.asarray(np.stack([np.broadcast_to(((np.arange(GLA_CHUNK) >> j) & 1)[:, None],
                                                  (GLA_CHUNK, GLA_KEY_DIM)) for j in range(FINE_LEVELS)]).astype(np.int32))
    low0, low1 = OFF_LOW, OFF_LOW + GLA_GATE_RANK
    rank = GLA_GATE_RANK
    h = x
    for l in range(depth):
        w_main = jnp.transpose(w_in[l])
        b_main = jnp.concatenate([b_in[l][:low0], b_in[l][low1:]])[None, :]
        w_low = jnp.pad(jnp.tile(jnp.transpose(w_main[low0:low1]).astype(BF16), (1, 3)),
                        ((0, 0), (0, 2 * LANES - 3 * rank)))
        b_low = jnp.pad(jnp.tile(b_in[l][low0:low1], 3), (0, 2 * LANES - 3 * rank))[None, :]
        wg_hi = gla_w_gate2[l].astype(BF16)
        wg_lo = (gla_w_gate2[l] - wg_hi.astype(F32)).astype(BF16)
        wg = jnp.pad(jnp.concatenate([wg_hi, wg_hi, wg_lo], axis=0), ((0, LANES - 3 * rank), (0, 0)))
        bs_plane = jnp.repeat(jnp.transpose(sg_b_s[l]), D_MODEL // SG_GROUPS, axis=1)
        consts = (w_main, b_main, w_low, b_low, wg, gla_b_gate[l][None, :],
                  sg_ln_g[l][None, :], sg_ln_b[l][None, :], sg_w_s[l], bs_plane, table, level, upper,
                  gla_norm_g[l][None, :], w_out[l], ln1_g[l][None, :], ln1_b[l][None, :])
        h, w1b, w2b = _mixer_call(h, w_ff1[l], w_ff2[l], consts, alpha)
        h = _ffn_call(h.reshape(bsz * t, d), w1b, w2b,
                      ln2_g[l][None, :], ln2_b[l][None, :], alpha).reshape(bsz, t, d)
    return h
```

```python
import functools
import math

import jax
import jax.numpy as jnp
import numpy as np
from jax.experimental import pallas as pl
from jax.experimental.pallas import tpu as pltpu

D_MODEL = 1024
SG_CHUNK = 128
SG_GROUPS = 8
GLA_HEADS = 4
GLA_KEY_DIM = 512
GLA_VAL_DIM = 1024
GLA_HEAD_K = GLA_KEY_DIM // GLA_HEADS
GLA_HEAD_V = GLA_VAL_DIM // GLA_HEADS
GLA_GATE_RANK = 16
GLA_GATE_TEMP = 16.0
D_FF = 4 * D_MODEL
LN_EPS = 1e-5

LANES = 128
BF16_ROWS = 16
GLA_CHUNK = 128
GLA_LEVELS = int(math.log2(GLA_CHUNK))
FINE_LEVELS = 3
MIX_BLOCK = 512
FFN_BLOCK = 1024
MIX_TAIL_ROWS = 256
FFN_TAIL_PIECES = 4
FFN_SLAB = 2048
W_STAGE_ROWS = 512
VMEM_LIMIT = 56 * 1024 * 1024

OFF_U, OFF_V, OFF_Q, OFF_K, OFF_GV, OFF_R, OFF_LOW, OFF_GA, OFF_GB = 0, 1024, 2048, 2560, 3072, 4096, 5120, 5136, 6160
D_MAIN = 7168

GELU_C = 0.7978845608028654
LOG2E = 1.4426950408889634

BF16 = jnp.bfloat16
F32 = jnp.float32


def _decay_tables():
    c = GLA_CHUNK
    blocks = []
    for lw in range(FINE_LEVELS - 1, -1, -1):
        a = np.zeros((c, c), np.float32)
        for t in range(c):
            if (t >> lw) & 1:
                a[t, (t >> lw) << lw:t + 1] = 1.0
            else:
                a[t, t + 1:((t >> lw) + 1) << lw] = 1.0
        blocks.append(a)
    blocks.append(np.tril(np.ones((c, c), np.float32)))
    table = np.concatenate(blocks, axis=0)
    t = np.arange(c)[:, None]
    s = np.arange(c)[None, :]
    x = t ^ s
    top = np.zeros_like(x)
    nz = x > 0
    top[nz] = np.floor(np.log2(x[nz])).astype(x.dtype)
    level = np.where(t > s, GLA_LEVELS - 1 - top, np.where(t == s, GLA_LEVELS, -1)).astype(np.int32)
    return table, level


def _layer_norm(x, g, b):
    mu = jnp.mean(x, axis=-1, keepdims=True)
    xc = x - mu
    var = jnp.mean(xc * xc, axis=-1, keepdims=True)
    return xc * jax.lax.rsqrt(var + LN_EPS) * g + b


def _gelu(x):
    hx = 0.5 * x
    return hx + hx * jnp.tanh(x * (GELU_C + (GELU_C * 0.044715) * (x * x)))


def _sigmoid(x):
    return 0.5 * jnp.tanh(0.5 * x) + 0.5


def _split_bf16(x):
    hi = x.astype(BF16)
    lo = (x - hi.astype(F32)).astype(BF16)
    return hi, lo


def _dot(a, b):
    return jnp.dot(a, b, preferred_element_type=F32)


def _dot_tn(a, b):
    return jax.lax.dot_general(a, b, (((0,), (0,)), ((), ())), preferred_element_type=F32)


def _cast_projection_weights(w_hbm, w_ref, stage_ref, sem):
    starts = [r for r in range(0, D_MAIN + GLA_GATE_RANK, W_STAGE_ROWS) if r < OFF_LOW]
    starts += list(range(OFF_LOW + GLA_GATE_RANK, D_MAIN + GLA_GATE_RANK, W_STAGE_ROWS))

    def copy(j):
        return pltpu.make_async_copy(w_hbm.at[pl.ds(starts[j], W_STAGE_ROWS)], stage_ref.at[j % 2], sem.at[j % 2])

    copy(0).start()
    for j, r0 in enumerate(starts):
        if j + 1 < len(starts):
            copy(j + 1).start()
        copy(j).wait()
        c0 = r0 if r0 < OFF_LOW else r0 - GLA_GATE_RANK
        w_ref[:, c0:c0 + W_STAGE_ROWS] = jnp.transpose(stage_ref[j % 2]).astype(BF16)


def _mixer_kernel(x_ref, w1f_ref, w2f_ref, w_hbm, b_ref, wlow_ref, blow_ref, wg_ref, bg_ref,
                  sg_g_ref, sg_b_ref, ws_ref, bs_ref, tab_ref, lvl_ref, upper_ref, gn_ref,
                  woutf_ref, ln_g_ref, ln_b_ref, o_ref, w1b_ref, w2b_ref,
                  state_ref, w_ref, wout_ref, stage_ref, sem, *, alpha):
    @pl.when((pl.program_id(0) == 0) & (pl.program_id(1) == 0))
    def _():
        _cast_projection_weights(w_hbm, w_ref, stage_ref, sem)
        wout_ref[...] = woutf_ref[...].astype(BF16)

    @pl.when(pl.program_id(1) == 0)
    def _():
        state_ref[...] = jnp.zeros_like(state_ref)

    w1b_ref[...] = w1f_ref[...].astype(BF16)
    w2b_ref[...] = w2f_ref[...].astype(BF16)

    x = x_ref[0]
    xh = x.astype(BF16)
    n_chunks = MIX_BLOCK // GLA_CHUNK

    def proj(off, width):
        c0 = off if off < OFF_LOW else off - GLA_GATE_RANK
        return _dot(xh, w_ref[:, c0:c0 + width]) + b_ref[:, c0:c0 + width]

    pv = proj(OFF_V, D_MODEL)
    q = proj(OFF_Q, GLA_KEY_DIM) * (GLA_HEAD_K ** -0.5)
    k = proj(OFF_K, GLA_KEY_DIM)
    a3 = (_dot(xh, wlow_ref[...]) + blow_ref[...])[:, :LANES]

    v_sg = jnp.concatenate(
        [_layer_norm(_gelu(pv[c * SG_CHUNK:(c + 1) * SG_CHUNK]), sg_g_ref[...], sg_b_ref[...]).astype(BF16)
         for c in range(MIX_BLOCK // SG_CHUNK)], axis=0)
    a3_hi = a3.astype(BF16).astype(F32)
    lane = jax.lax.broadcasted_iota(jnp.int32, a3.shape, 1)
    mid = (lane >= GLA_GATE_RANK) & (lane < 2 * GLA_GATE_RANK)
    z = _dot(jnp.where(mid, a3 - a3_hi, a3_hi).astype(BF16), wg_ref[...]) + bg_ref[...]
    pu = proj(OFF_U, D_MODEL)
    log_f = (jnp.minimum(z, 0.0) - jnp.log(1.0 + jnp.exp(-jnp.abs(z)))) * (LOG2E / GLA_GATE_TEMP)

    row = jax.lax.broadcasted_iota(jnp.int32, (SG_CHUNK, SG_CHUNK), 0)
    col = jax.lax.broadcasted_iota(jnp.int32, (SG_CHUNK, SG_CHUNK), 1)
    mixed_g = []
    for g in range(SG_GROUPS):
        w_g = jnp.where(row >= col, ws_ref[g], 0.0).astype(BF16)
        gs = slice(g * LANES, (g + 1) * LANES)
        rhs = jnp.concatenate([v_sg[c * SG_CHUNK:(c + 1) * SG_CHUNK, gs] for c in range(MIX_BLOCK // SG_CHUNK)],
                              axis=1)
        mixed_g.append(_dot(w_g, rhs))
    mixed = jnp.concatenate(
        [jnp.concatenate([mg[:, c * LANES:(c + 1) * LANES] for mg in mixed_g], axis=1) + bs_ref[...]
         for c in range(MIX_BLOCK // SG_CHUNK)], axis=0)
    pga = proj(OFF_GA, D_MODEL)

    decays, prefixes, suffixes = [], [], []
    for c in range(n_chunks):
        f_hi, f_lo = _split_bf16(log_f[c * GLA_CHUNK:(c + 1) * GLA_CHUNK])
        expo = _dot(tab_ref[...], jnp.concatenate([f_hi, f_lo], axis=0))
        cum = expo[FINE_LEVELS * GLA_CHUNK:]
        level_expo = []
        for lev in range(GLA_LEVELS - FINE_LEVELS):
            w = GLA_CHUNK >> (lev + 1)
            mid = jnp.concatenate([jnp.broadcast_to(cum[b + w - 1:b + w], (2 * w, GLA_KEY_DIM))
                                   for b in range(0, GLA_CHUNK, 2 * w)], axis=0)
            level_expo.append(-jnp.abs(cum - mid))
        level_expo.append(expo[:FINE_LEVELS * GLA_CHUNK])
        decays.append(jnp.exp2(jnp.concatenate(level_expo, axis=0)).astype(BF16))
        prefixes.append(jnp.exp2(cum))
        suffixes.append(jnp.exp2(cum[GLA_CHUNK - 1:GLA_CHUNK] - cum))

    pair_w = 2 * GLA_HEAD_K
    no_keys = jnp.zeros((GLA_HEAD_K, GLA_CHUNK), BF16)

    def pair_scores(lhs, keys_t):
        out = []
        for p in range(GLA_HEADS // 2):
            even = keys_t[(2 * p) * GLA_HEAD_K:(2 * p + 1) * GLA_HEAD_K]
            odd = keys_t[(2 * p + 1) * GLA_HEAD_K:(2 * p + 2) * GLA_HEAD_K]
            bd = jnp.concatenate([jnp.concatenate([even, no_keys], axis=1),
                                  jnp.concatenate([no_keys, odd], axis=1)], axis=0)
            out.append(_dot(lhs[:, p * pair_w:(p + 1) * pair_w], bd))
        return out

    def level_update(scores, c, lev):
        qc = q[c * GLA_CHUNK:(c + 1) * GLA_CHUNK]
        kc = k[c * GLA_CHUNK:(c + 1) * GLA_CHUNK]
        w = GLA_CHUNK >> (lev + 1)
        n_blk = GLA_CHUNK // w
        if w >= 8:
            mixed_qk = jnp.concatenate([(qc if (b % 2) else kc)[b * w:(b + 1) * w] for b in range(n_blk)], axis=0)
        else:
            mixed_qk = jnp.where(upper_ref[GLA_LEVELS - 1 - lev] != 0, qc, kc)
        zl = mixed_qk.astype(BF16) * decays[c][lev * GLA_CHUNK:(lev + 1) * GLA_CHUNK]
        zl_t = jnp.transpose(zl)
        if w >= BF16_ROWS:
            lhs = jnp.concatenate([zl[b * w:(b + 1) * w] for b in range(1, n_blk, 2)], axis=0)
            parts = pair_scores(lhs, zl_t)
            out = []
            for p in range(GLA_HEADS // 2):
                rows = []
                for b in range(n_blk):
                    old = scores[p][b * w:(b + 1) * w]
                    if b % 2:
                        new = parts[p][(b // 2) * w:(b // 2 + 1) * w]
                        old = jnp.where(lvl_ref[b * w:(b + 1) * w, :] == lev, new, old)
                    rows.append(old)
                out.append(jnp.concatenate(rows, axis=0))
            return out
        parts = pair_scores(zl, zl_t)
        return [jnp.where(lvl_ref[...] == lev, parts[p], scores[p]) for p in range(GLA_HEADS // 2)]

    def diagonal_update(scores, c):
        qk = q[c * GLA_CHUNK:(c + 1) * GLA_CHUNK] * k[c * GLA_CHUNK:(c + 1) * GLA_CHUNK]
        out = []
        for p in range(GLA_HEADS // 2):
            diag = [jnp.broadcast_to(jnp.sum(qk[:, h * GLA_HEAD_K:(h + 1) * GLA_HEAD_K], axis=1, keepdims=True),
                                     (GLA_CHUNK, GLA_CHUNK)) for h in (2 * p, 2 * p + 1)]
            out.append(jnp.where(lvl_ref[...] == GLA_LEVELS, jnp.concatenate(diag, axis=1), scores[p]))
        return out

    all_scores = [[jnp.zeros((GLA_CHUNK, 2 * GLA_CHUNK), F32) for _ in range(GLA_HEADS // 2)]
                  for _ in range(n_chunks)]
    m_acc = jnp.concatenate(
        [_sigmoid(pga[c * SG_CHUNK:(c + 1) * SG_CHUNK])
         * (_gelu(pu[c * SG_CHUNK:(c + 1) * SG_CHUNK]) * mixed[c * SG_CHUNK:(c + 1) * SG_CHUNK])
         for c in range(MIX_BLOCK // SG_CHUNK)], axis=0)
    for lev in range(GLA_LEVELS):
        all_scores = [level_update(all_scores[c], c, lev) for c in range(n_chunks)]
        if lev == 1:
            gv = proj(OFF_GV, GLA_VAL_DIM).astype(BF16)
        if lev == 4:
            r = proj(OFF_R, GLA_VAL_DIM)
    all_scores = [diagonal_update(all_scores[c], c) for c in range(n_chunks)]

    heads_sl = [slice(h * GLA_HEAD_K, (h + 1) * GLA_HEAD_K) for h in range(GLA_HEADS)]
    v_heads = [[gv[c * GLA_CHUNK:(c + 1) * GLA_CHUNK, h * GLA_HEAD_V:(h + 1) * GLA_HEAD_V]
                for h in range(GLA_HEADS)] for c in range(n_chunks)]
    kv, scale, q_ins = [], [], []
    for c in range(n_chunks):
        r0 = c * GLA_CHUNK
        q_ins.append((q[r0:r0 + GLA_CHUNK] * prefixes[c]).astype(BF16))
        k_out = (k[r0:r0 + GLA_CHUNK] * suffixes[c]).astype(BF16)
        total = prefixes[c][GLA_CHUNK - 1:GLA_CHUNK]
        kv.append([_dot_tn(k_out[:, sl], v_heads[c][h]) for h, sl in enumerate(heads_sl)])
        tcols = [jnp.transpose(jnp.broadcast_to(total[:, sl], (GLA_HEAD_K, GLA_HEAD_K))) for sl in heads_sl]
        scale.append([jnp.concatenate([t, t], axis=1) for t in tcols])
    states = [state_ref[h] for h in range(GLA_HEADS)]
    o_rows = []
    for c in range(n_chunks):
        heads = []
        for h, sl in enumerate(heads_sl):
            s_h = all_scores[c][h // 2][:, (h % 2) * GLA_CHUNK:(h % 2 + 1) * GLA_CHUNK].astype(BF16)
            o_h = _dot(jnp.concatenate([s_h, q_ins[c][:, sl]], axis=1),
                       jnp.concatenate([v_heads[c][h], states[h].astype(BF16)], axis=0))
            states[h] = states[h] * scale[c][h] + kv[c][h]
            o_h = o_h * jax.lax.rsqrt(jnp.mean(o_h * o_h, axis=-1, keepdims=True) + LN_EPS)
            heads.append(o_h)
        o_rows.append(jnp.concatenate(heads, axis=1))
    for h in range(GLA_HEADS):
        state_ref[h] = states[h]
    pgb = proj(OFF_GB, D_MODEL)
    m_rows = []
    for c in range(n_chunks):
        rs = slice(c * GLA_CHUNK, (c + 1) * GLA_CHUNK)
        y_b = (o_rows[c] * gn_ref[...]) * (r[rs] * _sigmoid(r[rs]))
        m_rows.append((m_acc[rs] + _sigmoid(pgb[rs]) * y_b).astype(BF16))
    m = jnp.concatenate(m_rows, axis=0)

    for r0, r1 in ((0, MIX_BLOCK - MIX_TAIL_ROWS), (MIX_BLOCK - MIX_TAIL_ROWS, MIX_BLOCK)):
        h1 = alpha * x_ref[0, r0:r1, :] + _dot(m[r0:r1], wout_ref[...])
        o_ref[0, r0:r1, :] = _layer_norm(h1, ln_g_ref[...], ln_b_ref[...])


def _ffn_kernel(h_ref, w1_ref, w2_ref, ln_g_ref, ln_b_ref, o_ref, *, alpha):
    h = h_ref[...]
    hb = h.astype(BF16)
    acc = alpha * h
    slab = FFN_SLAB

    def up(j):
        a = jnp.maximum(_dot(hb, w1_ref[:, j * slab:(j + 1) * slab]), 0.0)
        return (a * a).astype(BF16)

    n_slabs = D_FF // slab
    for j in range(n_slabs - 1):
        acc = acc + _dot(up(j), w2_ref[j * slab:(j + 1) * slab, :])
    a = up(n_slabs - 1)
    w2_last = w2_ref[(n_slabs - 1) * slab:, :]
    rows = FFN_BLOCK // FFN_TAIL_PIECES
    for r0 in range(0, FFN_BLOCK, rows):
        out = acc[r0:r0 + rows] + _dot(a[r0:r0 + rows], w2_last)
        o_ref[r0:r0 + rows, :] = _layer_norm(out, ln_g_ref[...], ln_b_ref[...])


def _resident(shape):
    nd = len(shape)
    return pl.BlockSpec(shape, lambda *_: (0,) * nd, pipeline_mode=pl.Buffered(1))


def _mixer_call(x, w1, w2, consts, alpha):
    bsz, t, d = x.shape
    blocks_per_row = t // MIX_BLOCK
    n_steps = bsz * blocks_per_row
    slab1, slab2 = w1.shape[0] // n_steps, w2.shape[0] // n_steps

    def slab(b, i):
        return (b * blocks_per_row + i, 0)

    return pl.pallas_call(
        functools.partial(_mixer_kernel, alpha=alpha),
        out_shape=(jax.ShapeDtypeStruct((bsz, t, d), F32),
                   jax.ShapeDtypeStruct(w1.shape, BF16), jax.ShapeDtypeStruct(w2.shape, BF16)),
        grid=(bsz, blocks_per_row),
        in_specs=[pl.BlockSpec((1, MIX_BLOCK, d), lambda b, i: (b, i, 0)),
                  pl.BlockSpec((slab1, w1.shape[1]), slab), pl.BlockSpec((slab2, w2.shape[1]), slab)]
        + [pl.BlockSpec(memory_space=pl.ANY)] + [_resident(c.shape) for c in consts[1:]],
        out_specs=(pl.BlockSpec((1, MIX_BLOCK, d), lambda b, i: (b, i, 0)),
                   pl.BlockSpec((slab1, w1.shape[1]), slab), pl.BlockSpec((slab2, w2.shape[1]), slab)),
        scratch_shapes=[pltpu.VMEM((GLA_HEADS, GLA_HEAD_K, GLA_HEAD_V), F32),
                        pltpu.VMEM((d, D_MAIN), BF16),
                        pltpu.VMEM((d, d), BF16),
                        pltpu.VMEM((2, W_STAGE_ROWS, d), F32),
                        pltpu.SemaphoreType.DMA((2,))],
        compiler_params=pltpu.CompilerParams(
            dimension_semantics=("arbitrary", "arbitrary"), vmem_limit_bytes=VMEM_LIMIT),
        name="mixer",
    )(x, w1, w2, *consts)


def _ffn_call(h, w1, w2, ln_g, ln_b, alpha):
    n, d = h.shape
    return pl.pallas_call(
        functools.partial(_ffn_kernel, alpha=alpha),
        out_shape=jax.ShapeDtypeStruct((n, d), F32),
        grid=(n // FFN_BLOCK,),
        in_specs=[pl.BlockSpec((FFN_BLOCK, d), lambda i: (i, 0)),
                  _resident(w1.shape), _resident(w2.shape), _resident(ln_g.shape), _resident(ln_b.shape)],
        out_specs=pl.BlockSpec((FFN_BLOCK, d), lambda i: (i, 0)),
        compiler_params=pltpu.CompilerParams(
            dimension_semantics=("arbitrary",), vmem_limit_bytes=VMEM_LIMIT),
        name="ffn",
    )(h, w1, w2, ln_g, ln_b)


def kernel(x, w_in, b_in, sg_ln_g, sg_ln_b, sg_w_s, sg_b_s, gla_w_gate2, gla_b_gate, gla_norm_g,
           w_out, ln1_g, ln1_b, w_ff1, w_ff2, ln2_g, ln2_b):
    bsz, t, d = x.shape
    depth = w_in.shape[0]
    alpha = (2.0 * depth) ** 0.25
    table_np, level_np = _decay_tables()
    table = jnp.asarray(np.concatenate([table_np, table_np], axis=1), BF16)
    level = jnp.asarray(np.concatenate([level_np, level_np], axis=1))
    upper = jnp.asarray(np.stack([np.broadcast_to(((np.arange(GLA_CHUNK) >> j) & 1)[:, None],
                                                  (GLA_CHUNK, GLA_KEY_DIM)) for j in range(FINE_LEVELS)]).astype(np.int32))
    low0, low1 = OFF_LOW, OFF_LOW + GLA_GATE_RANK
    rank = GLA_GATE_RANK
    h = x
    for l in range(depth):
        w_main = jnp.transpose(w_in[l])
        b_main = jnp.concatenate([b_in[l][:low0], b_in[l][low1:]])[None, :]
        w_low = jnp.pad(jnp.tile(jnp.transpose(w_main[low0:low1]).astype(BF16), (1, 3)),
                        ((0, 0), (0, 2 * LANES - 3 * rank)))
        b_low = jnp.pad(jnp.tile(b_in[l][low0:low1], 3), (0, 2 * LANES - 3 * rank))[None, :]
        wg_hi = gla_w_gate2[l].astype(BF16)
        wg_lo = (gla_w_gate2[l] - wg_hi.astype(F32)).astype(BF16)
        wg = jnp.pad(jnp.concatenate([wg_hi, wg_hi, wg_lo], axis=0), ((0, LANES - 3 * rank), (0, 0)))
        bs_plane = jnp.repeat(jnp.transpose(sg_b_s[l]), D_MODEL // SG_GROUPS, axis=1)
        consts = (w_main, b_main, w_low, b_low, wg, gla_b_gate[l][None, :],
                  sg_ln_g[l][None, :], sg_ln_b[l][None, :], sg_w_s[l], bs_plane, table, level, upper,
                  gla_norm_g[l][None, :], w_out[l], ln1_g[l][None, :], ln1_b[l][None, :])
        h, w1b, w2b = _mixer_call(h, w_ff1[l], w_ff2[l], consts, alpha)
        h = _ffn_call(h.reshape(bsz * t, d), w1b, w2b,
                      ln2_g[l][None, :], ln2_b[l][None, :], alpha).reshape(bsz, t, d)
    return h
```
